```python
import math
import jax, jax.numpy as jnp
from jax import lax
import numpy as np

D_MODEL = 1024
BATCH = 4
SEQ = 4096
DEPTH = 2
DEC_BATCH = 32
DEC_SEQ = 8
PAST_LEN = 16384
PAGE_SIZE = 128

N_A_LAYERS = DEPTH // 2
N_B_LAYERS = DEPTH - N_A_LAYERS
N_DENSE = (DEPTH + 1) // 2
N_MOE = DEPTH // 2
N_MEM = 256
MEM_HEADS = 4
MEM_HEAD_DIM = 64
MEM_W = MEM_HEADS * MEM_HEAD_DIM
CONV_DIM = D_MODEL - MEM_W
CONV_WIDTH = 3
DIFF_HEAD_DIM = 64
DIFF_V_DIM = 2 * DIFF_HEAD_DIM
DIFF_HEADS = (D_MODEL - MEM_W) // DIFF_V_DIM
DIFF_QK_W = DIFF_HEADS * 2 * DIFF_HEAD_DIM
DIFF_V_W = DIFF_HEADS * DIFF_V_DIM
Q_BLOCK = 128
ROPE_THETA = 10000.0
D_FF = ((8 * D_MODEL // 3 + 127) // 128) * 128
N_EXPERTS = 8
TOP_K = 2
D_FF_EXPERT = D_FF // 2
MOE_BLOCK = 128
EPS = 1e-6

kernel_name = 'yoco_shortconv_diffattn_memxattn_moe_step'


def rmsnorm(x, g):
    xf = x.astype(jnp.float32)
    xf = xf * lax.rsqrt(jnp.mean(xf * xf, axis=-1, keepdims=True) + EPS)
    return xf.astype(x.dtype) * g


def rope(x, pos):
    half = x.shape[-1] // 2
    inv = ROPE_THETA ** (-jnp.arange(half, dtype=jnp.float32) / half)
    ang = pos.astype(jnp.float32)[:, None] * inv[None, :]
    shape = (1, ang.shape[0]) + (1,) * (x.ndim - 3) + (half,)
    cos = jnp.cos(ang).reshape(shape)
    sin = jnp.sin(ang).reshape(shape)
    xf = x.astype(jnp.float32)
    x1, x2 = xf[..., :half], xf[..., half:]
    return jnp.concatenate([x1 * cos - x2 * sin, x2 * cos + x1 * sin], axis=-1).astype(x.dtype)


def short_conv(u, prev, w):
    t = u.shape[1]
    ue = jnp.concatenate([prev, u], axis=1)
    y = sum(w[j] * ue[:, j:j + t] for j in range(CONV_WIDTH))
    return y, ue[:, t:]


def swiglu(x, wg, wu, wd):
    return (jax.nn.silu(x @ wg) * (x @ wu)) @ wd


def moe_ffn(x, router, wg, wu, wd):
    shp = x.shape
    d = shp[-1]
    xt = x.reshape(-1, d)
    n = xt.shape[0]
    logits = jnp.dot(xt, router, preferred_element_type=jnp.float32)
    top_v, top_e = lax.top_k(logits, TOP_K)
    gates = jax.nn.softmax(top_v, axis=-1).astype(x.dtype)
    a = n * TOP_K
    flat_e = top_e.reshape(-1)
    flat_tok = jnp.arange(a, dtype=jnp.int32) // TOP_K
    flat_g = gates.reshape(-1)
    order = jnp.argsort(flat_e, stable=True)
    se = flat_e[order]
    counts = jnp.bincount(flat_e, length=N_EXPERTS).astype(jnp.int32)
    padded = (counts + MOE_BLOCK - 1) // MOE_BLOCK * MOE_BLOCK
    cum_pad = jnp.cumsum(padded)
    start_pad = cum_pad - padded
    start = jnp.cumsum(counts) - counts
    dest = start_pad[se] + jnp.arange(a, dtype=jnp.int32) - start[se]
    n_rows = -(-(a + N_EXPERTS * (MOE_BLOCK - 1)) // MOE_BLOCK) * MOE_BLOCK
    n_blk = n_rows // MOE_BLOCK
    buf_tok = jnp.full((n_rows,), n, jnp.int32).at[dest].set(flat_tok[order])
    buf_g = jnp.zeros((n_rows,), x.dtype).at[dest].set(flat_g[order])
    blk_e = jnp.minimum(jnp.searchsorted(cum_pad, jnp.arange(n_blk, dtype=jnp.int32) * MOE_BLOCK, side='right'), N_EXPERTS - 1)
    xp = jnp.concatenate([xt, jnp.zeros((1, d), x.dtype)], axis=0)
    xb = xp[buf_tok].reshape(n_blk, MOE_BLOCK, d)

    def expert_rows(args):
        xr, e = args
        return swiglu(xr, wg[e], wu[e], wd[e])

    yb = lax.map(expert_rows, (xb, blk_e)).reshape(n_rows, d)
    y = jnp.zeros((n + 1, d), x.dtype).at[buf_tok].add(yb * buf_g[:, None])
    return y[:n].reshape(shp)


def mem_attention(q, mk, mv):
    s = jnp.einsum('bthd,bmhd->bhtm', q, mk, preferred_element_type=jnp.float32) * (MEM_HEAD_DIM ** -0.5)
    p = jax.nn.softmax(s, axis=-1).astype(mv.dtype)
    return jnp.einsum('bhtm,bmhd->bthd', p, mv)


def mem_kv_proj(mem, g_mem, w_mem_kv):
    b, m = mem.shape[0], mem.shape[1]
    ks, vs = [], []
    for l in range(DEPTH):
        kv = rmsnorm(mem, g_mem[l]) @ w_mem_kv[l]
        ks.append(kv[..., :MEM_W].reshape(b, m, MEM_HEADS, MEM_HEAD_DIM))
        vs.append(kv[..., MEM_W:].reshape(b, m, MEM_HEADS, MEM_HEAD_DIM))
    return jnp.stack(ks), jnp.stack(vs)


def diff_scores(q, k):
    return jnp.einsum('bqhcd,bkhcd->bchqk', q, k, preferred_element_type=jnp.float32) * (DIFF_HEAD_DIM ** -0.5)


def diff_weights(s, lam):
    p = jax.nn.softmax(s, axis=-1)
    return p[:, 0] - lam * p[:, 1]


def diff_attention_prompt(q, k, v, lam):
    b, t = q.shape[0], q.shape[1]
    nb = t // Q_BLOCK
    qb = jnp.moveaxis(q.reshape(b, nb, Q_BLOCK, DIFF_HEADS, 2, DIFF_HEAD_DIM), 1, 0)
    kpos = jnp.arange(t)

    def one_block(args):
        qi, i = args
        s = diff_scores(qi, k)
        qpos = i * Q_BLOCK + jnp.arange(Q_BLOCK)
        mask = kpos[None, :] <= qpos[:, None]
        a = diff_weights(jnp.where(mask, s, -jnp.inf), lam).astype(v.dtype)
        return jnp.einsum('bhqk,bkhe->bqhe', a, v)

    o = lax.map(one_block, (qb, jnp.arange(nb)))
    return jnp.moveaxis(o, 0, 1).reshape(b, t, DIFF_HEADS, DIFF_V_DIM)


def diff_attention_sample(q, k_new, v_new, k_past, v_past, lam):
    t = q.shape[1]
    n_past = k_past.shape[1]
    s_past = diff_scores(q, k_past)
    s_new = diff_scores(q, k_new)
    causal = jnp.arange(t)[None, :] <= jnp.arange(t)[:, None]
    s_new = jnp.where(causal, s_new, -jnp.inf)
    a = diff_weights(jnp.concatenate([s_past, s_new], axis=-1), lam).astype(v_new.dtype)
    return (jnp.einsum('bhqk,bkhe->bqhe', a[..., :n_past], v_past)
            + jnp.einsum('bhqk,bkhe->bqhe', a[..., n_past:], v_new))


def run_trunk(h, pos, conv_prev, mem_k, mem_v, attend, p):
    b, t, _ = h.shape
    conv_new = []
    k_sh = None
    v_sh = None
    for l in range(DEPTH):
        hn = rmsnorm(h, p['g_mix'][l])
        if l < N_A_LAYERS:
            proj = hn @ p['a_w_in'][l]
            gate_b = proj[..., :CONV_DIM]
            gate_c = proj[..., CONV_DIM:2 * CONV_DIM]
            hv = proj[..., 2 * CONV_DIM:3 * CONV_DIM]
            qm = proj[..., 3 * CONV_DIM:]
            y_conv, st = short_conv(gate_c * hv, conv_prev[l], p['a_conv'][l])
            conv_new.append(st)
            mix = gate_b * y_conv
            w_out = p['a_w_out'][l]
        else:
            j = l - N_A_LAYERS
            lam_init = 0.8 - 0.6 * math.exp(-0.3 * l)
            lp = p['b_lambda'][j].astype(jnp.float32)
            lam = jnp.exp(jnp.sum(lp[0] * lp[1])) - jnp.exp(jnp.sum(lp[2] * lp[3])) + lam_init
            proj = hn @ p['b_w_in'][j]
            q = rope(proj[..., :DIFF_QK_W].reshape(b, t, DIFF_HEADS, 2, DIFF_HEAD_DIM), pos)
            qm = proj[..., DIFF_QK_W:]
            o = attend(q, k_sh, v_sh, lam)
            mix = (rmsnorm(o, p['b_subln'][j]) * (1.0 - lam_init)).reshape(b, t, DIFF_V_W)
            w_out = p['b_w_out'][j]
        mo = mem_attention(qm.reshape(b, t, MEM_HEADS, MEM_HEAD_DIM), mem_k[l], mem_v[l]).reshape(b, t, MEM_W)
        h = h + jnp.concatenate([mix, mo], axis=-1) @ w_out
        hn = rmsnorm(h, p['g_ffn'][l])
        if l % 2 == 0:
            i = l // 2
            h = h + swiglu(hn, p['f_w_gate'][i], p['f_w_up'][i], p['f_w_down'][i])
        else:
            i = l // 2
            h = h + moe_ffn(hn, p['m_router'][i], p['m_w_gate'][i], p['m_w_up'][i], p['m_w_down'][i])
        if l == N_A_LAYERS - 1:
            kv = rmsnorm(h, p['g_kv']) @ p['w_kv']
            k_sh = rope(kv[..., :DIFF_QK_W].reshape(b, t, DIFF_HEADS, 2, DIFF_HEAD_DIM), pos)
            v_sh = kv[..., DIFF_QK_W:].reshape(b, t, DIFF_HEADS, DIFF_V_DIM)
    return rmsnorm(h, p['g_final']), jnp.stack(conv_new), k_sh, v_sh


def setup_inputs(seed: int = 0) -> dict:
    key = jax.random.key(seed)
    ks = jax.random.split(key, 32)
    f32 = jnp.float32
    n_pages = PAST_LEN // PAGE_SIZE
    n_used = DEC_BATCH * n_pages
    n_phys = n_used + max(1, n_used // 4)

    def nrm(k, shape, scale=1.0):
        return jax.random.normal(k, shape, f32) * scale

    def gain(k, shape):
        return 1.0 + 0.02 * jax.random.normal(k, shape, f32)

    page_table = jax.random.permutation(ks[0], n_phys)[:n_used].reshape(DEC_BATCH, n_pages).astype(jnp.int32)
    return {
        'x_prompt': nrm(ks[1], (BATCH, SEQ, D_MODEL)),
        'x_sample': nrm(ks[2], (DEC_BATCH, DEC_SEQ, D_MODEL)),
        'state_conv': nrm(ks[3], (N_A_LAYERS, DEC_BATCH, CONV_WIDTH - 1, CONV_DIM)),
        'cache_k': nrm(ks[4], (n_phys, PAGE_SIZE, DIFF_HEADS, 2 * DIFF_HEAD_DIM)),
        'cache_v': nrm(ks[5], (n_phys, PAGE_SIZE, DIFF_HEADS, DIFF_V_DIM)),
        'cache_mem_k': nrm(ks[6], (DEPTH, DEC_BATCH, N_MEM, MEM_HEADS, MEM_HEAD_DIM)),
        'cache_mem_v': nrm(ks[7], (DEPTH, DEC_BATCH, N_MEM, MEM_HEADS, MEM_HEAD_DIM)),
        'page_table': page_table,
        'mem_prompt': nrm(ks[8], (BATCH, N_MEM, D_MODEL)),
        'g_mix': gain(ks[9], (DEPTH, D_MODEL)),
        'g_ffn': gain(ks[10], (DEPTH, D_MODEL)),
        'g_mem': gain(ks[11], (DEPTH, D_MODEL)),
        'w_mem_kv': nrm(ks[12], (DEPTH, D_MODEL, 2 * MEM_W), D_MODEL ** -0.5),
        'a_w_in': nrm(ks[13], (N_A_LAYERS, D_MODEL, 3 * CONV_DIM + MEM_W), D_MODEL ** -0.5),
        'a_conv': nrm(ks[14], (N_A_LAYERS, CONV_WIDTH, CONV_DIM), CONV_WIDTH ** -0.5),
        'a_w_out': nrm(ks[15], (N_A_LAYERS, CONV_DIM + MEM_W, D_MODEL), (CONV_DIM + MEM_W) ** -0.5),
        'g_kv': gain(ks[16], (D_MODEL,)),
        'w_kv': nrm(ks[17], (D_MODEL, DIFF_QK_W + DIFF_V_W), D_MODEL ** -0.5),
        'b_w_in': nrm(ks[18], (N_B_LAYERS, D_MODEL, DIFF_QK_W + MEM_W), D_MODEL ** -0.5),
        'b_lambda': nrm(ks[19], (N_B_LAYERS, 4, DIFF_HEAD_DIM), 0.1),
        'b_subln': gain(ks[20], (N_B_LAYERS, DIFF_V_DIM)),
        'b_w_out': nrm(ks[21], (N_B_LAYERS, DIFF_V_W + MEM_W, D_MODEL), (DIFF_V_W + MEM_W) ** -0.5),
        'f_w_gate': nrm(ks[22], (N_DENSE, D_MODEL, D_FF), D_MODEL ** -0.5),
        'f_w_up': nrm(ks[23], (N_DENSE, D_MODEL, D_FF), D_MODEL ** -0.5),
        'f_w_down': nrm(ks[24], (N_DENSE, D_FF, D_MODEL), D_FF ** -0.5),
        'm_router': nrm(ks[25], (N_MOE, D_MODEL, N_EXPERTS), D_MODEL ** -0.5),
        'm_w_gate': nrm(ks[26], (N_MOE, N_EXPERTS, D_MODEL, D_FF_EXPERT), D_MODEL ** -0.5),
        'm_w_up': nrm(ks[27], (N_MOE, N_EXPERTS, D_MODEL, D_FF_EXPERT), D_MODEL ** -0.5),
        'm_w_down': nrm(ks[28], (N_MOE, N_EXPERTS, D_FF_EXPERT, D_MODEL), D_FF_EXPERT ** -0.5),
        'g_final': gain(ks[29], (D_MODEL,)),
    }


def reference(x_prompt, x_sample, state_conv, cache_k, cache_v, cache_mem_k, cache_mem_v, page_table,
              mem_prompt, g_mix, g_ffn, g_mem, w_mem_kv, a_w_in, a_conv, a_w_out, g_kv, w_kv,
              b_w_in, b_lambda, b_subln, b_w_out, f_w_gate, f_w_up, f_w_down,
              m_router, m_w_gate, m_w_up, m_w_down, g_final):
    params = {
        'g_mix': g_mix, 'g_ffn': g_ffn, 'a_w_in': a_w_in, 'a_conv': a_conv, 'a_w_out': a_w_out,
        'g_kv': g_kv, 'w_kv': w_kv, 'b_w_in': b_w_in, 'b_lambda': b_lambda, 'b_subln': b_subln,
        'b_w_out': b_w_out, 'f_w_gate': f_w_gate, 'f_w_up': f_w_up, 'f_w_down': f_w_down,
        'm_router': m_router, 'm_w_gate': m_w_gate, 'm_w_up': m_w_up, 'm_w_down': m_w_down,
        'g_final': g_final,
    }
    bp, tp = x_prompt.shape[0], x_prompt.shape[1]
    pos_p = jnp.arange(tp)
    mem_k_p, mem_v_p = mem_kv_proj(mem_prompt, g_mem, w_mem_kv)
    conv0 = jnp.zeros((N_A_LAYERS, bp, CONV_WIDTH - 1, CONV_DIM), x_prompt.dtype)

    def attend_prompt(q, k, v, lam):
        return diff_attention_prompt(q, k, v, lam)

    y_p, conv_p, k_p, v_p = run_trunk(x_prompt, pos_p, conv0, mem_k_p, mem_v_p, attend_prompt, params)

    bs, ts = x_sample.shape[0], x_sample.shape[1]
    past = page_table.shape[1] * cache_k.shape[1]
    k_past = cache_k[page_table].reshape(bs, past, DIFF_HEADS, 2, DIFF_HEAD_DIM)
    v_past = cache_v[page_table].reshape(bs, past, DIFF_HEADS, DIFF_V_DIM)
    pos_s = past + jnp.arange(ts)

    def attend_sample(q, k, v, lam):
        return diff_attention_sample(q, k, v, k_past, v_past, lam)

    y_s, conv_s, k_s, v_s = run_trunk(x_sample, pos_s, state_conv, cache_mem_k, cache_mem_v, attend_sample, params)

    k_p = k_p.reshape(bp, tp, DIFF_HEADS, 2 * DIFF_HEAD_DIM)
    k_s = k_s.reshape(bs, ts, DIFF_HEADS, 2 * DIFF_HEAD_DIM)
    return (y_p, y_s, conv_p, conv_s, k_p, v_p, k_s, v_s, mem_k_p, mem_v_p)
```

```python
import functools
import math

import jax
import jax.numpy as jnp
from jax import lax
from jax.experimental import pallas as pl
from jax.experimental.pallas import tpu as pltpu

F32 = jnp.float32
BF16 = jnp.bfloat16

EPS = 1e-6
MEM_HEADS = 4
MEM_HEAD_DIM = 64
MEM_W = MEM_HEADS * MEM_HEAD_DIM
HEAD_W = 128
HALF_W = HEAD_W // 2
ROPE_HALF = HALF_W // 2
ROPE_THETA = 10000.0
CONV_WIDTH = 3
TOP_K = 2
NEG = -1e30
QK_SCALE = HALF_W ** -0.5
MEM_SCALE = MEM_HEAD_DIM ** -0.5
LANES = 128
SUBLANES = 8
VMEM_LIMIT = 56 * 1024 * 1024


def _cparams(*sem):
    return pltpu.CompilerParams(dimension_semantics=sem, vmem_limit_bytes=VMEM_LIMIT)


def _rms(x, g):
    return x * lax.rsqrt(jnp.mean(x * x, axis=-1, keepdims=True) + EPS) * g


def _dot(a, b):
    return jnp.dot(a, b, preferred_element_type=F32)


def _dot_t(a, b):
    return lax.dot_general(a, b, (((1,), (1,)), ((), ())), preferred_element_type=F32)


def _resident(shape):
    n = len(shape)
    return pl.BlockSpec(shape, lambda *_: (0,) * n, pipeline_mode=pl.Buffered(1))


def _mem_attn(qm, mk, mv):
    q = (qm * MEM_SCALE).astype(BF16)
    kb = mk.astype(BF16)
    vb = mv.astype(BF16)
    lane = lax.broadcasted_iota(jnp.int32, q.shape, 1)
    out = jnp.zeros(q.shape, F32)
    for h in range(MEM_HEADS):
        in_head = (lane >= h * MEM_HEAD_DIM) & (lane < (h + 1) * MEM_HEAD_DIM)
        s = _dot_t(jnp.where(in_head, q, jnp.zeros_like(q)), kb)
        m = jnp.max(s, axis=-1, keepdims=True)
        p = jnp.exp(s - m)
        p = p / jnp.sum(p, axis=-1, keepdims=True)
        out = jnp.where(in_head, _dot(p.astype(BF16), vb), out)
    return out


def _rope_slab(x, cos, sin_signed):
    lane = lax.broadcasted_iota(jnp.int32, x.shape, 1)
    first_half = (lane % HALF_W) < ROPE_HALF
    partner = jnp.where(first_half, pltpu.roll(x, HEAD_W - ROPE_HALF, 1), pltpu.roll(x, ROPE_HALF, 1))
    return x * cos + partner * sin_signed


def _lam(lam_ref, lam_init):
    lp = lam_ref[...]
    a = jnp.sum(lp[0:1] * lp[1:2], axis=-1, keepdims=True)
    b = jnp.sum(lp[2:3] * lp[3:4], axis=-1, keepdims=True)
    return jnp.exp(a) - jnp.exp(b) + lam_init


def _subln(o, g, lam_init):
    o = o * lax.rsqrt(jnp.mean(o * o, axis=-1, keepdims=True) + EPS)
    return o * g * (1.0 - lam_init)


def _memkv_kernel(x_ref, g_ref, w_ref, o_ref):
    hn = _rms(x_ref[...], g_ref[0]).astype(BF16)
    o_ref[0] = _dot(hn, w_ref[0])


def _memkv(mem2d, g_mem, w_bf):
    depth, d, n = w_bf.shape
    m = mem2d.shape[0]
    return pl.pallas_call(
        _memkv_kernel,
        grid=(depth,),
        in_specs=[
            pl.BlockSpec((m, d), lambda l: (0, 0)),
            pl.BlockSpec((1, 1, d), lambda l: (l, 0, 0)),
            pl.BlockSpec((1, d, n), lambda l: (l, 0, 0)),
        ],
        out_specs=pl.BlockSpec((1, m, n), lambda l: (l, 0, 0)),
        out_shape=jax.ShapeDtypeStruct((depth, m, n), F32),
        compiler_params=_cparams("arbitrary"),
        name="memkv",
    )(mem2d, g_mem.reshape(depth, 1, d), w_bf)


def _mixa_kernel(x_ref, g_ref, win_ref, cw_ref, wout_ref, mk_ref, mv_ref, prev_ref, h_ref, st_ref, ubuf,
                 *, tm, c, n_t):
    i = pl.program_id(1)
    x = x_ref[...]
    hn = _rms(x, g_ref[...]).astype(BF16)
    proj = _dot(hn, win_ref[...])
    gate_b = proj[:, :c]
    u = proj[:, c:2 * c] * proj[:, 2 * c:3 * c]
    qm = proj[:, 3 * c:]

    @pl.when(i == 0)
    def _():
        ubuf[SUBLANES - 2:SUBLANES, :] = prev_ref[0]

    @pl.when(i > 0)
    def _():
        ubuf[SUBLANES - 2:SUBLANES, :] = ubuf[tm + SUBLANES - 2:tm + SUBLANES, :]

    ubuf[SUBLANES:SUBLANES + tm, :] = u
    cw = cw_ref[...]
    y = cw[0:1] * ubuf[SUBLANES - 2:SUBLANES - 2 + tm, :]
    y = y + cw[1:2] * ubuf[SUBLANES - 1:SUBLANES - 1 + tm, :]
    y = y + cw[2:3] * u
    mix = gate_b * y
    mo = _mem_attn(qm, mk_ref[0], mv_ref[0])
    out = _dot(mix.astype(BF16), wout_ref[:c, :]) + _dot(mo.astype(BF16), wout_ref[c:, :])
    h_ref[...] = x + out

    @pl.when(i == n_t - 1)
    def _():
        st_ref[0] = ubuf[tm + SUBLANES - 2:tm + SUBLANES, :]


def _mixa(x2d, b, t, tm, g, win_bf, conv_w, wout_bf, mem_k, mem_v, prev):
    d = x2d.shape[1]
    c = d - MEM_W
    n_t = t // tm
    n_mem = mem_k.shape[1]
    kern = functools.partial(_mixa_kernel, tm=tm, c=c, n_t=n_t)
    return pl.pallas_call(
        kern,
        grid=(b, n_t),
        in_specs=[
            pl.BlockSpec((tm, d), lambda bi, i: (bi * n_t + i, 0)),
            _resident((1, d)),
            _resident(win_bf.shape),
            _resident(conv_w.shape),
            _resident(wout_bf.shape),
            pl.BlockSpec((1, n_mem, MEM_W), lambda bi, i: (bi, 0, 0)),
            pl.BlockSpec((1, n_mem, MEM_W), lambda bi, i: (bi, 0, 0)),
            pl.BlockSpec((1, CONV_WIDTH - 1, c), lambda bi, i: (bi, 0, 0)),
        ],
        out_specs=[
            pl.BlockSpec((tm, d), lambda bi, i: (bi * n_t + i, 0)),
            pl.BlockSpec((1, CONV_WIDTH - 1, c), lambda bi, i: (bi, 0, 0)),
        ],
        out_shape=[
            jax.ShapeDtypeStruct(x2d.shape, F32),
            jax.ShapeDtypeStruct((b, CONV_WIDTH - 1, c), F32),
        ],
        scratch_shapes=[pltpu.VMEM((tm + SUBLANES, c), F32)],
        compiler_params=_cparams("arbitrary", "arbitrary"),
        name="mixa",
    )(x2d, g.reshape(1, d), win_bf, conv_w, wout_bf, mem_k, mem_v, prev)


def _silu(g):
    return g / (1.0 + jnp.exp(-g))


def _ffn_kernel(h_ref, g_ref, wg_ref, wu_ref, wd_ref, o_ref):
    h = h_ref[...]
    hn = _rms(h, g_ref[...]).astype(BF16)
    a = _silu(_dot(hn, wg_ref[...])) * _dot(hn, wu_ref[...])
    o_ref[...] = h + _dot(a.astype(BF16), wd_ref[...])


def _ffn(h2d, tm, g, wg_bf, wu_bf, wd_bf):
    m, d = h2d.shape
    return pl.pallas_call(
        _ffn_kernel,
        grid=(m // tm,),
        in_specs=[
            pl.BlockSpec((tm, d), lambda i: (i, 0)),
            _resident((1, d)),
            _resident(wg_bf.shape),
            _resident(wu_bf.shape),
            _resident(wd_bf.shape),
        ],
        out_specs=pl.BlockSpec((tm, d), lambda i: (i, 0)),
        out_shape=jax.ShapeDtypeStruct(h2d.shape, F32),
        compiler_params=_cparams("arbitrary"),
        name="ffn",
    )(h2d, g.reshape(1, d), wg_bf, wu_bf, wd_bf)


def _kv_kernel(h_ref, g_ref, w_ref, cos_ref, sin_ref, k_ref, v_ref, kb_ref, vb_ref, *, n_heads):
    hn = _rms(h_ref[...], g_ref[...]).astype(BF16)
    kv = _dot(hn, w_ref[...])
    cos = cos_ref[...]
    sin = sin_ref[...]
    for h in range(n_heads):
        sl = slice(h * HEAD_W, (h + 1) * HEAD_W)
        r = _rope_slab(kv[:, sl], cos, sin)
        k_ref[:, sl] = r
        kb_ref[:, sl] = r.astype(BF16)
    v = kv[:, n_heads * HEAD_W:]
    v_ref[...] = v
    vb_ref[...] = v.astype(BF16)


def _q_kernel(h_ref, g_ref, w_ref, cos_ref, sin_ref, q_ref, qm_ref, *, n_heads):
    hn = _rms(h_ref[...], g_ref[...]).astype(BF16)
    p = _dot(hn, w_ref[...])
    cos = cos_ref[...]
    sin = sin_ref[...]
    for h in range(n_heads):
        sl = slice(h * HEAD_W, (h + 1) * HEAD_W)
        q_ref[:, sl] = (_rope_slab(p[:, sl], cos, sin) * QK_SCALE).astype(BF16)
    qm_ref[...] = p[:, n_heads * HEAD_W:]


def _rope_tables(pos, tm):
    inv = ROPE_THETA ** (-jnp.arange(ROPE_HALF, dtype=F32) / ROPE_HALF)
    ang = pos.astype(F32)[:, None] * inv[None, :]
    cos = jnp.tile(jnp.cos(ang), (1, HEAD_W // ROPE_HALF))
    sin = jnp.sin(ang)
    sin = jnp.tile(jnp.concatenate([-sin, sin], axis=-1), (1, HEAD_W // HALF_W))
    t = pos.shape[0]
    if tm > t:
        cos = jnp.tile(cos, (tm // t, 1))
        sin = jnp.tile(sin, (tm // t, 1))
    return cos, sin


def _rope_spec(t, tm):
    n_t = max(t // tm, 1)
    return pl.BlockSpec((tm, HEAD_W), lambda i: (i % n_t, 0))


def _kvproj(h2d, t, tm, g, w_bf, cos, sin):
    m, d = h2d.shape
    n_heads = w_bf.shape[1] // (2 * HEAD_W)
    w = n_heads * HEAD_W
    kern = functools.partial(_kv_kernel, n_heads=n_heads)
    row = lambda i: (i, 0)
    return pl.pallas_call(
        kern,
        grid=(m // tm,),
        in_specs=[
            pl.BlockSpec((tm, d), row),
            _resident((1, d)),
            _resident(w_bf.shape),
            _rope_spec(t, tm),
            _rope_spec(t, tm),
        ],
        out_specs=[pl.BlockSpec((tm, w), row)] * 4,
        out_shape=[
            jax.ShapeDtypeStruct((m, w), F32),
            jax.ShapeDtypeStruct((m, w), F32),
            jax.ShapeDtypeStruct((m, w), BF16),
            jax.ShapeDtypeStruct((m, w), BF16),
        ],
        compiler_params=_cparams("arbitrary"),
        name="kvproj",
    )(h2d, g.reshape(1, d), w_bf, cos, sin)


def _qproj(h2d, t, tm, g, w_bf, cos, sin):
    m, d = h2d.shape
    n_heads = (w_bf.shape[1] - MEM_W) // HEAD_W
    w = n_heads * HEAD_W
    kern = functools.partial(_q_kernel, n_heads=n_heads)
    row = lambda i: (i, 0)
    return pl.pallas_call(
        kern,
        grid=(m // tm,),
        in_specs=[
            pl.BlockSpec((tm, d), row),
            _resident((1, d)),
            _resident(w_bf.shape),
            _rope_spec(t, tm),
            _rope_spec(t, tm),
        ],
        out_specs=[pl.BlockSpec((tm, w), row), pl.BlockSpec((tm, MEM_W), row)],
        out_shape=[jax.ShapeDtypeStruct((m, w), BF16), jax.ShapeDtypeStruct((m, MEM_W), F32)],
        compiler_params=_cparams("arbitrary"),
        name="qproj",
    )(h2d, g.reshape(1, d), w_bf, cos, sin)


def _stack_components(q):
    lane = lax.broadcasted_iota(jnp.int32, q.shape, 1)
    zero = jnp.zeros_like(q)
    return jnp.concatenate([jnp.where(lane < HALF_W, q, zero), jnp.where(lane >= HALF_W, q, zero)], axis=0)


def _online_update(s, v_bf, m_ref, l_ref, acc_ref):
    m_prev = m_ref[...]
    m_new = jnp.maximum(m_prev, jnp.max(s, axis=-1, keepdims=True))
    alpha = jnp.exp(m_prev - m_new)
    p = jnp.exp(s - m_new)
    l_ref[...] = alpha * l_ref[...] + jnp.sum(p, axis=-1, keepdims=True)
    acc_ref[...] = alpha * acc_ref[...] + _dot(p.astype(BF16), v_bf)
    m_ref[...] = m_new


def _diff_combine(m_ref, l_ref, acc_ref, t, lam):
    o = acc_ref[...] / l_ref[...]
    return o[:t] - lam * o[t:]


def _attn_prompt_kernel(q_ref, k_ref, v_ref, lam_ref, sg_ref, o_ref, m_ref, l_ref, acc_ref, *, tq, lam_init):
    qi = pl.program_id(2)
    qs = _stack_components(q_ref[...])
    m_ref[...] = jnp.full(m_ref.shape, NEG, F32)
    l_ref[...] = jnp.zeros(l_ref.shape, F32)
    acc_ref[...] = jnp.zeros(acc_ref.shape, F32)

    def chunk(j, masked):
        start = pl.multiple_of(j * tq, tq)
        s = _dot_t(qs, k_ref[pl.ds(start, tq), :])
        if masked:
            row = lax.broadcasted_iota(jnp.int32, s.shape, 0) % tq
            col = lax.broadcasted_iota(jnp.int32, s.shape, 1)
            s = jnp.where(col <= row, s, NEG)
        _online_update(s, v_ref[pl.ds(start, tq), :], m_ref, l_ref, acc_ref)

    def body(j, carry):
        chunk(j, False)
        return carry

    lax.fori_loop(0, qi, body, 0)
    chunk(qi, True)
    o = _diff_combine(m_ref, l_ref, acc_ref, tq, _lam(lam_ref, lam_init))
    o_ref[...] = _subln(o, sg_ref[...], lam_init).astype(BF16)


def _attn_prompt(q_bf, k_bf, v_bf, b, t, tq, lam_p, subln_g, lam_init):
    m, w = q_bf.shape
    n_heads = w // HEAD_W
    nq = t // tq
    kern = functools.partial(_attn_prompt_kernel, tq=tq, lam_init=lam_init)
    return pl.pallas_call(
        kern,
        grid=(b, n_heads, nq),
        in_specs=[
            pl.BlockSpec((tq, HEAD_W), lambda bi, h, qi: (bi * nq + qi, h)),
            pl.BlockSpec((t, HEAD_W), lambda bi, h, qi: (bi, h)),
            pl.BlockSpec((t, HEAD_W), lambda bi, h, qi: (bi, h)),
            pl.BlockSpec(lam_p.shape, lambda bi, h, qi: (0, 0)),
            pl.BlockSpec((1, HEAD_W), lambda bi, h, qi: (0, 0)),
        ],
        out_specs=pl.BlockSpec((tq, HEAD_W), lambda bi, h, qi: (bi * nq + qi, h)),
        out_shape=jax.ShapeDtypeStruct((m, w), BF16),
        scratch_shapes=[
            pltpu.VMEM((2 * tq, 1), F32),
            pltpu.VMEM((2 * tq, 1), F32),
            pltpu.VMEM((2 * tq, HEAD_W), F32),
        ],
        compiler_params=_cparams("arbitrary", "arbitrary", "arbitrary"),
        name="attn_prompt",
    )(q_bf, k_bf, v_bf, lam_p, subln_g.reshape(1, HEAD_W))


def _attn_sample_kernel(pt_ref, q_ref, kn_ref, vn_ref, *rest, ts, n_heads, pages, n_groups, lam_init):
    k_refs = rest[:pages]
    v_refs = rest[pages:2 * pages]
    lam_ref, sg_ref, o_ref, qs_ref, m_ref, l_ref, acc_ref = rest[2 * pages:]
    g = pl.program_id(1)

    @pl.when(g == 0)
    def _():
        for h in range(n_heads):
            qs_ref[h] = _stack_components(q_ref[:, h * HEAD_W:(h + 1) * HEAD_W])
        m_ref[...] = jnp.full(m_ref.shape, NEG, F32)
        l_ref[...] = jnp.zeros(l_ref.shape, F32)
        acc_ref[...] = jnp.zeros(acc_ref.shape, F32)

    for h in range(n_heads):
        kh = jnp.concatenate([r[0, :, h, :].astype(BF16) for r in k_refs], axis=0)
        vh = jnp.concatenate([r[0, :, h, :].astype(BF16) for r in v_refs], axis=0)
        s = _dot_t(qs_ref[h], kh)
        _online_update(s, vh, m_ref.at[h], l_ref.at[h], acc_ref.at[h])

    @pl.when(g == n_groups - 1)
    def _():
        lam = _lam(lam_ref, lam_init)
        for h in range(n_heads):
            sl = slice(h * HEAD_W, (h + 1) * HEAD_W)
            s = _dot_t(qs_ref[h], kn_ref[:, sl].astype(BF16))
            row = lax.broadcasted_iota(jnp.int32, s.shape, 0) % ts
            col = lax.broadcasted_iota(jnp.int32, s.shape, 1)
            s = jnp.where(col <= row, s, NEG)
            _online_update(s, vn_ref[:, sl].astype(BF16), m_ref.at[h], l_ref.at[h], acc_ref.at[h])
            o = _diff_combine(m_ref.at[h], l_ref.at[h], acc_ref.at[h], ts, lam)
            o_ref[:, sl] = _subln(o, sg_ref[...], lam_init).astype(BF16)


def _attn_sample(q_bf, k_new, v_new, cache_k, cache_v, page_table, bs, ts, pages, lam_p, subln_g, lam_init):
    m, w = q_bf.shape
    n_heads = w // HEAD_W
    ps = cache_k.shape[1]
    n_pages = page_table.shape[1]
    n_groups = n_pages // pages
    kern = functools.partial(_attn_sample_kernel, ts=ts, n_heads=n_heads, pages=pages, n_groups=n_groups,
                             lam_init=lam_init)

    def page_spec(i):
        return pl.BlockSpec((1, ps, n_heads, HEAD_W), lambda bi, g, pt: (pt[bi, g * pages + i], 0, 0, 0))

    row = lambda bi, g, pt: (bi, 0)
    grid_spec = pltpu.PrefetchScalarGridSpec(
        num_scalar_prefetch=1,
        grid=(bs, n_groups),
        in_specs=[pl.BlockSpec((ts, w), row)] * 3
        + [page_spec(i) for i in range(pages)] * 2
        + [pl.BlockSpec(lam_p.shape, lambda bi, g, pt: (0, 0)), pl.BlockSpec((1, HEAD_W), lambda bi, g, pt: (0, 0))],
        out_specs=pl.BlockSpec((ts, w), row),
        scratch_shapes=[
            pltpu.VMEM((n_heads, 2 * ts, HEAD_W), BF16),
            pltpu.VMEM((n_heads, 2 * ts, 1), F32),
            pltpu.VMEM((n_heads, 2 * ts, 1), F32),
            pltpu.VMEM((n_heads, 2 * ts, HEAD_W), F32),
        ],
    )
    return pl.pallas_call(
        kern,
        grid_spec=grid_spec,
        out_shape=jax.ShapeDtypeStruct((m, w), BF16),
        compiler_params=_cparams("arbitrary", "arbitrary"),
        name="attn_sample",
    )(page_table, q_bf, k_new, v_new, *([cache_k] * pages), *([cache_v] * pages), lam_p, subln_g.reshape(1, HEAD_W))


def _mixb_kernel(mix_ref, qm_ref, x_ref, wout_ref, mk_ref, mv_ref, h_ref):
    c = mix_ref.shape[1]
    mo = _mem_attn(qm_ref[...], mk_ref[0], mv_ref[0])
    out = _dot(mix_ref[...], wout_ref[:c, :]) + _dot(mo.astype(BF16), wout_ref[c:, :])
    h_ref[...] = x_ref[...] + out


def _mixb(mix_bf, qm, x2d, b, t, tm, wout_bf, mem_k, mem_v):
    m, d = x2d.shape
    c = mix_bf.shape[1]
    n_t = t // tm
    n_mem = mem_k.shape[1]
    row = lambda bi, i: (bi * n_t + i, 0)
    return pl.pallas_call(
        _mixb_kernel,
        grid=(b, n_t),
        in_specs=[
            pl.BlockSpec((tm, c), row),
            pl.BlockSpec((tm, MEM_W), row),
            pl.BlockSpec((tm, d), row),
            _resident(wout_bf.shape),
            pl.BlockSpec((1, n_mem, MEM_W), lambda bi, i: (bi, 0, 0)),
            pl.BlockSpec((1, n_mem, MEM_W), lambda bi, i: (bi, 0, 0)),
        ],
        out_specs=pl.BlockSpec((tm, d), row),
        out_shape=jax.ShapeDtypeStruct((m, d), F32),
        compiler_params=_cparams("arbitrary", "arbitrary"),
        name="mixb",
    )(mix_bf, qm, x2d, wout_bf, mem_k, mem_v)


def _router_kernel(h_ref, g_ref, r_ref, o_ref, *, n_experts):
    hn = _rms(h_ref[...], g_ref[...])
    logits = jnp.dot(hn, r_ref[...], preferred_element_type=F32, precision=lax.Precision.HIGHEST)
    lane = lax.broadcasted_iota(jnp.int32, logits.shape, 1)
    logits = jnp.where(lane < n_experts, logits, -jnp.inf)
    v1 = jnp.max(logits, axis=-1, keepdims=True)
    e1 = jnp.min(jnp.where(logits == v1, lane, LANES), axis=-1, keepdims=True)
    rest = jnp.where(lane == e1, -jnp.inf, logits)
    v2 = jnp.max(rest, axis=-1, keepdims=True)
    e2 = jnp.min(jnp.where(rest == v2, lane, LANES), axis=-1, keepdims=True)
    ex = jnp.exp(v2 - v1)
    g1 = 1.0 / (1.0 + ex)
    g2 = ex / (1.0 + ex)
    out = jnp.where(lane == 0, e1.astype(F32), 0.0)
    out = jnp.where(lane == 1, e2.astype(F32), out)
    out = jnp.where(lane == 2, g1, out)
    out = jnp.where(lane == 3, g2, out)
    o_ref[...] = out


def _router(h2d, tm, g, router):
    m, d = h2d.shape
    n_experts = router.shape[1]
    r_pad = jnp.zeros((d, LANES), F32).at[:, :n_experts].set(router)
    kern = functools.partial(_router_kernel, n_experts=n_experts)
    return pl.pallas_call(
        kern,
        grid=(m // tm,),
        in_specs=[pl.BlockSpec((tm, d), lambda i: (i, 0)), _resident((1, d)), _resident((d, LANES))],
        out_specs=pl.BlockSpec((tm, LANES), lambda i: (i, 0)),
        out_shape=jax.ShapeDtypeStruct((m, LANES), F32),
        compiler_params=_cparams("arbitrary"),
        name="router",
    )(h2d, g.reshape(1, d), r_pad)


def _row_copy(src_hbm, row, dst, slot, sem):
    return pltpu.make_async_copy(src_hbm.at[pl.ds(row, 1)], dst.at[pl.ds(slot, 1)], sem)


def _expert_kernel(blk_e_ref, tok_ref, nused_ref, h_hbm, g_ref, wg_ref, wu_ref, wd_ref, o_ref, xbuf, sem, *, blk):
    i = pl.program_id(0)

    @pl.when(i < nused_ref[0])
    def _():
        def issue(r, carry):
            _row_copy(h_hbm, tok_ref[i * blk + r], xbuf, r, sem).start()
            return carry

        lax.fori_loop(0, blk, issue, 0)

        def wait(r, carry):
            _row_copy(h_hbm, 0, xbuf, r, sem).wait()
            return carry

        lax.fori_loop(0, blk, wait, 0)
        hn = _rms(xbuf[...], g_ref[...]).astype(BF16)
        a = _silu(_dot(hn, wg_ref[0])) * _dot(hn, wu_ref[0])
        o_ref[...] = _dot(a.astype(BF16), wd_ref[0])

    @pl.when(i >= nused_ref[0])
    def _():
        o_ref[...] = jnp.zeros(o_ref.shape, F32)


def _experts(h2d, g, wg_bf, wu_bf, wd_bf, blk_e, buf_tok, n_used, blk):
    m, d = h2d.shape
    n_rows = buf_tok.shape[0]
    n_blk = n_rows // blk
    ff = wg_bf.shape[2]
    kern = functools.partial(_expert_kernel, blk=blk)
    grid_spec = pltpu.PrefetchScalarGridSpec(
        num_scalar_prefetch=3,
        grid=(n_blk,),
        in_specs=[
            pl.BlockSpec(memory_space=pl.ANY),
            pl.BlockSpec((1, d), lambda i, be, tk, nu: (0, 0)),
            pl.BlockSpec((1, d, ff), lambda i, be, tk, nu: (be[i], 0, 0)),
            pl.BlockSpec((1, d, ff), lambda i, be, tk, nu: (be[i], 0, 0)),
            pl.BlockSpec((1, ff, d), lambda i, be, tk, nu: (be[i], 0, 0)),
        ],
        out_specs=pl.BlockSpec((blk, d), lambda i, be, tk, nu: (i, 0)),
        scratch_shapes=[pltpu.VMEM((blk, d), F32), pltpu.SemaphoreType.DMA(())],
    )
    return pl.pallas_call(
        kern,
        grid_spec=grid_spec,
        out_shape=jax.ShapeDtypeStruct((n_rows, d), F32),
        compiler_params=_cparams("arbitrary"),
        name="experts",
    )(blk_e, buf_tok, n_used, h2d, g.reshape(1, d), wg_bf, wu_bf, wd_bf)


def _combine_kernel(pos_ref, h_ref, route_ref, g_ref, yb_hbm, o_ref, ybuf, sem, *, tc):
    i = pl.program_id(0)

    def issue(r, carry):
        for k in range(TOP_K):
            _row_copy(yb_hbm, pos_ref[(i * tc + r) * TOP_K + k], ybuf.at[k], r, sem).start()
        return carry

    lax.fori_loop(0, tc, issue, 0)

    def wait(r, carry):
        for k in range(TOP_K):
            _row_copy(yb_hbm, 0, ybuf.at[k], r, sem).wait()
        return carry

    lax.fori_loop(0, tc, wait, 0)
    route = route_ref[...]
    y = ybuf[0] * route[:, 2:3] + ybuf[1] * route[:, 3:4]
    o_ref[...] = _rms(h_ref[...] + y, g_ref[...])


def _combine(h2d, route, yb, pos, g, tc):
    m, d = h2d.shape
    kern = functools.partial(_combine_kernel, tc=tc)
    grid_spec = pltpu.PrefetchScalarGridSpec(
        num_scalar_prefetch=1,
        grid=(m // tc,),
        in_specs=[
            pl.BlockSpec((tc, d), lambda i, p: (i, 0)),
            pl.BlockSpec((tc, LANES), lambda i, p: (i, 0)),
            pl.BlockSpec((1, d), lambda i, p: (0, 0)),
            pl.BlockSpec(memory_space=pl.ANY),
        ],
        out_specs=pl.BlockSpec((tc, d), lambda i, p: (i, 0)),
        scratch_shapes=[pltpu.VMEM((TOP_K, tc, d), F32), pltpu.SemaphoreType.DMA(())],
    )
    return pl.pallas_call(
        kern,
        grid_spec=grid_spec,
        out_shape=jax.ShapeDtypeStruct((m, d), F32),
        compiler_params=_cparams("arbitrary"),
        name="combine",
    )(pos, h2d, route, g.reshape(1, d), yb)


def _moe(h2d, tm, g_ffn, router, wg_bf, wu_bf, wd_bf, g_final, blk):
    m, d = h2d.shape
    n_experts = router.shape[1]
    route = _router(h2d, tm, g_ffn, router)
    flat_e = route[:, :TOP_K].astype(jnp.int32).reshape(-1)
    a = flat_e.shape[0]
    onehot = (flat_e[:, None] == jnp.arange(n_experts, dtype=jnp.int32)[None, :]).astype(jnp.int32)
    csum = jnp.cumsum(onehot, axis=0)
    counts = csum[-1]
    rank = jnp.take_along_axis(csum, flat_e[:, None], axis=1)[:, 0] - 1
    padded = (counts + blk - 1) // blk * blk
    cum_pad = jnp.cumsum(padded)
    dest = (cum_pad - padded)[flat_e] + rank
    n_blk = -(-(a + n_experts * (blk - 1)) // blk)
    n_rows = n_blk * blk
    buf_tok = jnp.zeros((n_rows,), jnp.int32).at[dest].set(jnp.arange(a, dtype=jnp.int32) // TOP_K)
    n_used = (cum_pad[-1] // blk).astype(jnp.int32)
    blk_i = jnp.minimum(jnp.arange(n_blk, dtype=jnp.int32), n_used - 1)
    blk_e = jnp.minimum(jnp.searchsorted(cum_pad, blk_i * blk, side='right'), n_experts - 1).astype(jnp.int32)
    yb = _experts(h2d, g_ffn, wg_bf, wu_bf, wd_bf, blk_e, buf_tok, n_used.reshape(1), blk)
    return _combine(h2d, route, yb, dest, g_final, tm)


def _trunk(x, pos, conv_prev, mem_k, mem_v, attend, w, tm_mix, tm_tok, moe_blk):
    b, t, d = x.shape
    x2d = x.reshape(b * t, d)
    c = d - MEM_W
    n_mem = mem_k.shape[2]
    mk = mem_k.reshape(mem_k.shape[0], b, n_mem, MEM_W)
    mv = mem_v.reshape(mem_v.shape[0], b, n_mem, MEM_W)
    h, conv_st = _mixa(x2d, b, t, tm_mix, w['g_mix'][0], w['a_w_in'], w['a_conv'], w['a_w_out'], mk[0], mv[0],
                       conv_prev)
    h = _ffn(h, tm_tok, w['g_ffn'][0], w['f_w_gate'], w['f_w_up'], w['f_w_down'])
    cos, sin = _rope_tables(pos, tm_tok)
    k, v, k_bf, v_bf = _kvproj(h, t, tm_tok, w['g_kv'], w['w_kv'], cos, sin)
    q_bf, qm = _qproj(h, t, tm_tok, w['g_mix'][1], w['b_w_in'], cos, sin)
    mix = attend(q_bf, k, v, k_bf, v_bf)
    h = _mixb(mix, qm, h, b, t, tm_mix, w['b_w_out'], mk[1], mv[1])
    y = _moe(h, tm_tok, w['g_ffn'][1], w['m_router'], w['m_w_gate'], w['m_w_up'], w['m_w_down'], w['g_final'],
             moe_blk)
    n_heads = c // HEAD_W
    return (y.reshape(b, t, d), conv_st, k.reshape(b, t, n_heads, HEAD_W), v.reshape(b, t, n_heads, HEAD_W))


def kernel(x_prompt, x_sample, state_conv, cache_k, cache_v, cache_mem_k, cache_mem_v, page_table, mem_prompt, g_mix, g_ffn, g_mem, w_mem_kv, a_w_in, a_conv, a_w_out, g_kv, w_kv, b_w_in, b_lambda, b_subln, b_w_out, f_w_gate, f_w_up, f_w_down, m_router, m_w_gate, m_w_up, m_w_down, g_final):
    bp, tp, d = x_prompt.shape
    bs, ts, _ = x_sample.shape
    depth = g_mix.shape[0]
    n_a = a_w_in.shape[0]
    assert depth == 2 and n_a == 1 and b_w_in.shape[0] == 1, "one conv layer followed by one attention layer"
    assert ts == SUBLANES and tp % 256 == 0
    c = d - MEM_W
    lam_init = 0.8 - 0.6 * math.exp(-0.3 * n_a)
    w = {
        'g_mix': g_mix, 'g_ffn': g_ffn, 'g_kv': g_kv, 'g_final': g_final,
        'a_w_in': a_w_in[0].astype(BF16), 'a_conv': a_conv[0], 'a_w_out': a_w_out[0].astype(BF16),
        'w_kv': w_kv.astype(BF16), 'b_w_in': b_w_in[0].astype(BF16), 'b_w_out': b_w_out[0].astype(BF16),
        'f_w_gate': f_w_gate[0].astype(BF16), 'f_w_up': f_w_up[0].astype(BF16), 'f_w_down': f_w_down[0].astype(BF16),
        'm_router': m_router[0], 'm_w_gate': m_w_gate[0].astype(BF16), 'm_w_up': m_w_up[0].astype(BF16),
        'm_w_down': m_w_down[0].astype(BF16),
    }
    lam_p = b_lambda[0]
    subln_g = b_subln[0]

    n_mem = mem_prompt.shape[1]
    kv_mem = _memkv(mem_prompt.reshape(bp * n_mem, d), g_mem, w_mem_kv.astype(BF16))
    mem_k_p = kv_mem[..., :MEM_W].reshape(depth, bp, n_mem, MEM_HEADS, MEM_HEAD_DIM)
    mem_v_p = kv_mem[..., MEM_W:].reshape(depth, bp, n_mem, MEM_HEADS, MEM_HEAD_DIM)
    tq = 256

    def attend_prompt(q_bf, k, v, k_bf, v_bf):
        return _attn_prompt(q_bf, k_bf, v_bf, bp, tp, tq, lam_p, subln_g, lam_init)

    y_p, conv_p, k_p, v_p = _trunk(
        x_prompt, jnp.arange(tp), jnp.zeros((bp, CONV_WIDTH - 1, c), F32), mem_k_p, mem_v_p, attend_prompt, w,
        tm_mix=256, tm_tok=256, moe_blk=256)

    past = page_table.shape[1] * cache_k.shape[1]
    pages = math.gcd(page_table.shape[1], 8)

    def attend_sample(q_bf, k, v, k_bf, v_bf):
        return _attn_sample(q_bf, k, v, cache_k, cache_v, page_table, bs, ts, pages, lam_p, subln_g, lam_init)

    y_s, conv_s, k_s, v_s = _trunk(
        x_sample, past + jnp.arange(ts), state_conv[0], cache_mem_k, cache_mem_v, attend_sample, w,
        tm_mix=ts, tm_tok=bs * ts, moe_blk=128)

    return (y_p, y_s, conv_p[None], conv_s[None], k_p, v_p, k_s, v_s, mem_k_p, mem_v_p)
```

```python
import functools
import math

import jax
import jax.numpy as jnp
from jax import lax
from jax.experimental import pallas as pl
from jax.experimental.pallas import tpu as pltpu

F32 = jnp.float32
BF16 = jnp.bfloat16

EPS = 1e-6
MEM_HEADS = 4
MEM_HEAD_DIM = 64
MEM_W = MEM_HEADS * MEM_HEAD_DIM
HEAD_W = 128
HALF_W = HEAD_W // 2
ROPE_HALF = HALF_W // 2
ROPE_THETA = 10000.0
CONV_WIDTH = 3
TOP_K = 2
NEG = -1e30
QK_SCALE = HALF_W ** -0.5 * math.log2(math.e)
MEM_SCALE = MEM_HEAD_DIM ** -0.5
LANES = 128
SUBLANES = 8
VMEM_LIMIT = 56 * 1024 * 1024


def _cparams(*sem):
    return pltpu.CompilerParams(dimension_semantics=sem, vmem_limit_bytes=VMEM_LIMIT)


def _rms(x, g):
    return x * lax.rsqrt(jnp.mean(x * x, axis=-1, keepdims=True) + EPS) * g


def _dot(a, b):
    return jnp.dot(a, b, preferred_element_type=F32)


def _dot_t(a, b):
    return lax.dot_general(a, b, (((1,), (1,)), ((), ())), preferred_element_type=F32)


def _resident(shape):
    n = len(shape)
    return pl.BlockSpec(shape, lambda *_: (0,) * n, pipeline_mode=pl.Buffered(1))


def _mem_attn(qm, mk_t, mv_t):
    q = (qm * MEM_SCALE).astype(BF16)
    kb = mk_t.astype(BF16)
    vb = mv_t.astype(BF16)
    lane = lax.broadcasted_iota(jnp.int32, q.shape, 1)
    out = jnp.zeros(q.shape, F32)
    for h in range(MEM_HEADS):
        in_head = (lane >= h * MEM_HEAD_DIM) & (lane < (h + 1) * MEM_HEAD_DIM)
        s = _dot(jnp.where(in_head, q, jnp.zeros_like(q)), kb)
        m = jnp.max(s, axis=-1, keepdims=True)
        p = jnp.exp(s - m)
        p = p / jnp.sum(p, axis=-1, keepdims=True)
        out = jnp.where(in_head, _dot_t(p.astype(BF16), vb), out)
    return out


def _rope_slab(x, cos, sin_signed):
    lane = lax.broadcasted_iota(jnp.int32, x.shape, 1)
    first_half = (lane % HALF_W) < ROPE_HALF
    partner = jnp.where(first_half, pltpu.roll(x, HEAD_W - ROPE_HALF, 1), pltpu.roll(x, ROPE_HALF, 1))
    return x * cos + partner * sin_signed


def _lam(lam_ref, lam_init):
    lp = lam_ref[...]
    a = jnp.sum(lp[0:1] * lp[1:2], axis=-1, keepdims=True)
    b = jnp.sum(lp[2:3] * lp[3:4], axis=-1, keepdims=True)
    return jnp.exp(a) - jnp.exp(b) + lam_init


def _subln(o, g, lam_init):
    o = o * lax.rsqrt(jnp.mean(o * o, axis=-1, keepdims=True) + EPS)
    return o * g * (1.0 - lam_init)


def _memkv_kernel(x_ref, g_ref, w_ref, k_ref, v_ref):
    hn = _rms(x_ref[0], g_ref[0]).astype(BF16)
    kv_t = _dot(hn, w_ref[0]).T
    k_ref[0, 0] = kv_t[:MEM_W]
    v_ref[0, 0] = kv_t[MEM_W:]


def _memkv(mem, g_mem, w_bf):
    depth, d, n = w_bf.shape
    b, n_mem, _ = mem.shape
    out = jax.ShapeDtypeStruct((depth, b, MEM_W, n_mem), F32)
    return pl.pallas_call(
        _memkv_kernel,
        grid=(depth, b),
        in_specs=[
            pl.BlockSpec((1, n_mem, d), lambda l, bi: (bi, 0, 0)),
            pl.BlockSpec((1, 1, d), lambda l, bi: (l, 0, 0)),
            pl.BlockSpec((1, d, n), lambda l, bi: (l, 0, 0)),
        ],
        out_specs=[pl.BlockSpec((1, 1, MEM_W, n_mem), lambda l, bi: (l, bi, 0, 0))] * 2,
        out_shape=[out, out],
        compiler_params=_cparams("arbitrary", "arbitrary"),
        name="memkv",
    )(mem, g_mem.reshape(depth, 1, d), w_bf)


def _mixa_kernel(x_ref, g_ref, win_ref, cw_ref, wout_ref, mk_ref, mv_ref, prev_ref, h_ref, st_ref, ubuf,
                 *, tm, c, n_t):
    i = pl.program_id(1)
    x = x_ref[...]
    hn = _rms(x, g_ref[...]).astype(BF16)
    proj = _dot(hn, win_ref[...])
    gate_b = proj[:, :c]
    u = proj[:, c:2 * c] * proj[:, 2 * c:3 * c]
    qm = proj[:, 3 * c:]

    @pl.when(i == 0)
    def _():
        ubuf[SUBLANES - 2:SUBLANES, :] = prev_ref[0]

    @pl.when(i > 0)
    def _():
        ubuf[SUBLANES - 2:SUBLANES, :] = ubuf[tm + SUBLANES - 2:tm + SUBLANES, :]

    ubuf[SUBLANES:SUBLANES + tm, :] = u
    cw = cw_ref[...]
    y = cw[0:1] * ubuf[SUBLANES - 2:SUBLANES - 2 + tm, :]
    y = y + cw[1:2] * ubuf[SUBLANES - 1:SUBLANES - 1 + tm, :]
    y = y + cw[2:3] * u
    mix = gate_b * y
    mo = _mem_attn(qm, mk_ref[0], mv_ref[0])
    out = _dot(mix.astype(BF16), wout_ref[:c, :]) + _dot(mo.astype(BF16), wout_ref[c:, :])
    h_ref[...] = x + out

    @pl.when(i == n_t - 1)
    def _():
        st_ref[0] = ubuf[tm + SUBLANES - 2:tm + SUBLANES, :]


def _mem_spec(n_mem):
    return pl.BlockSpec((1, MEM_W, n_mem), lambda bi, i: (bi, 0, 0))


def _mixa(x2d, b, t, tm, g, win_bf, conv_w, wout_bf, mem_k, mem_v, prev):
    d = x2d.shape[1]
    c = d - MEM_W
    n_t = t // tm
    n_mem = mem_k.shape[2]
    kern = functools.partial(_mixa_kernel, tm=tm, c=c, n_t=n_t)
    return pl.pallas_call(
        kern,
        grid=(b, n_t),
        in_specs=[
            pl.BlockSpec((tm, d), lambda bi, i: (bi * n_t + i, 0)),
            _resident((1, d)),
            _resident(win_bf.shape),
            _resident(conv_w.shape),
            _resident(wout_bf.shape),
            _mem_spec(n_mem),
            _mem_spec(n_mem),
            pl.BlockSpec((1, CONV_WIDTH - 1, c), lambda bi, i: (bi, 0, 0)),
        ],
        out_specs=[
            pl.BlockSpec((tm, d), lambda bi, i: (bi * n_t + i, 0)),
            pl.BlockSpec((1, CONV_WIDTH - 1, c), lambda bi, i: (bi, 0, 0)),
        ],
        out_shape=[
            jax.ShapeDtypeStruct(x2d.shape, F32),
            jax.ShapeDtypeStruct((b, CONV_WIDTH - 1, c), F32),
        ],
        scratch_shapes=[pltpu.VMEM((tm + SUBLANES, c), F32)],
        compiler_params=_cparams("arbitrary", "arbitrary"),
        name="mixa",
    )(x2d, g.reshape(1, d), win_bf, conv_w, wout_bf, mem_k, mem_v, prev)


def _silu(g):
    return g / (1.0 + jnp.exp(-g))


def _ffn_kernel(h_ref, g_ref, wg_ref, wu_ref, wd_ref, o_ref):
    h = h_ref[...]
    hn = _rms(h, g_ref[...]).astype(BF16)
    a = _silu(_dot(hn, wg_ref[...])) * _dot(hn, wu_ref[...])
    o_ref[...] = h + _dot(a.astype(BF16), wd_ref[...])


def _ffn(h2d, tm, g, wg_bf, wu_bf, wd_bf):
    m, d = h2d.shape
    return pl.pallas_call(
        _ffn_kernel,
        grid=(m // tm,),
        in_specs=[
            pl.BlockSpec((tm, d), lambda i: (i, 0)),
            _resident((1, d)),
            _resident(wg_bf.shape),
            _resident(wu_bf.shape),
            _resident(wd_bf.shape),
        ],
        out_specs=pl.BlockSpec((tm, d), lambda i: (i, 0)),
        out_shape=jax.ShapeDtypeStruct(h2d.shape, F32),
        compiler_params=_cparams("arbitrary"),
        name="ffn",
    )(h2d, g.reshape(1, d), wg_bf, wu_bf, wd_bf)


def _kv_kernel(h_ref, g_ref, w_ref, cos_ref, sin_ref, k_ref, v_ref, kb_ref, vb_ref, *, n_heads):
    hn = _rms(h_ref[...], g_ref[...]).astype(BF16)
    kv = _dot(hn, w_ref[...])
    cos = cos_ref[...]
    sin = sin_ref[...]
    for h in range(n_heads):
        r = _rope_slab(kv[:, h * HEAD_W:(h + 1) * HEAD_W], cos, sin)
        k_ref[0, h] = r
        kb_ref[0, h] = r.astype(BF16)
        v = kv[:, (n_heads + h) * HEAD_W:(n_heads + h + 1) * HEAD_W]
        v_ref[0, h] = v
        vb_ref[0, h] = v.astype(BF16)


def _q_kernel(h_ref, g_ref, w_ref, cos_ref, sin_ref, q_ref, qm_ref, *, n_heads):
    hn = _rms(h_ref[...], g_ref[...]).astype(BF16)
    p = _dot(hn, w_ref[...])
    cos = cos_ref[...]
    sin = sin_ref[...]
    for h in range(n_heads):
        q_ref[0, h] = (_rope_slab(p[:, h * HEAD_W:(h + 1) * HEAD_W], cos, sin) * QK_SCALE).astype(BF16)
    qm_ref[...] = p[:, n_heads * HEAD_W:]


def _rope_tables(pos):
    inv = ROPE_THETA ** (-jnp.arange(ROPE_HALF, dtype=F32) / ROPE_HALF)
    ang = pos.astype(F32)[:, None] * inv[None, :]
    cos = jnp.tile(jnp.cos(ang), (1, HEAD_W // ROPE_HALF))
    sin = jnp.sin(ang)
    sin = jnp.tile(jnp.concatenate([-sin, sin], axis=-1), (1, HEAD_W // HALF_W))
    return cos, sin


def _kvproj(h2d, b, t, tm, g, w_bf, cos, sin):
    m, d = h2d.shape
    n_heads = w_bf.shape[1] // (2 * HEAD_W)
    n_t = t // tm
    kern = functools.partial(_kv_kernel, n_heads=n_heads)
    head_major = pl.BlockSpec((1, n_heads, tm, HEAD_W), lambda bi, i: (bi, 0, i, 0))
    rope = pl.BlockSpec((tm, HEAD_W), lambda bi, i: (i, 0))
    return pl.pallas_call(
        kern,
        grid=(b, n_t),
        in_specs=[pl.BlockSpec((tm, d), lambda bi, i: (bi * n_t + i, 0)), _resident((1, d)), _resident(w_bf.shape),
                  rope, rope],
        out_specs=[head_major] * 4,
        out_shape=[jax.ShapeDtypeStruct((b, n_heads, t, HEAD_W), F32)] * 2
        + [jax.ShapeDtypeStruct((b, n_heads, t, HEAD_W), BF16)] * 2,
        compiler_params=_cparams("arbitrary", "arbitrary"),
        name="kvproj",
    )(h2d, g.reshape(1, d), w_bf, cos, sin)


def _qproj(h2d, b, t, tm, g, w_bf, cos, sin):
    m, d = h2d.shape
    n_heads = (w_bf.shape[1] - MEM_W) // HEAD_W
    n_t = t // tm
    kern = functools.partial(_q_kernel, n_heads=n_heads)
    rope = pl.BlockSpec((tm, HEAD_W), lambda bi, i: (i, 0))
    return pl.pallas_call(
        kern,
        grid=(b, n_t),
        in_specs=[pl.BlockSpec((tm, d), lambda bi, i: (bi * n_t + i, 0)), _resident((1, d)), _resident(w_bf.shape),
                  rope, rope],
        out_specs=[pl.BlockSpec((1, n_heads, tm, HEAD_W), lambda bi, i: (bi, 0, i, 0)),
                   pl.BlockSpec((tm, MEM_W), lambda bi, i: (bi * n_t + i, 0))],
        out_shape=[jax.ShapeDtypeStruct((b, n_heads, t, HEAD_W), BF16), jax.ShapeDtypeStruct((m, MEM_W), F32)],
        compiler_params=_cparams("arbitrary", "arbitrary"),
        name="qproj",
    )(h2d, g.reshape(1, d), w_bf, cos, sin)


def _stack_components(q):
    lane = lax.broadcasted_iota(jnp.int32, q.shape, 1)
    zero = jnp.zeros_like(q)
    return jnp.concatenate([jnp.where(lane < HALF_W, q, zero), jnp.where(lane >= HALF_W, q, zero)], axis=0)


def _online_update(s, v_bf, m_ref, l_ref, acc_ref):
    cols = [s[:, c * LANES:(c + 1) * LANES] for c in range(s.shape[1] // LANES)]
    m_prev = m_ref[...]
    m_new = jnp.maximum(m_prev, jnp.max(functools.reduce(jnp.maximum, cols), axis=-1, keepdims=True))
    alpha = jnp.exp2(m_prev - m_new)
    ps = [jnp.exp2(c - m_new) for c in cols]
    l_ref[...] = alpha * l_ref[...] + functools.reduce(jnp.add, ps)
    p = jnp.concatenate(ps, axis=1).astype(BF16)
    acc_ref[...] = alpha * acc_ref[...] + _dot(p, v_bf)
    m_ref[...] = m_new


def _init_online(m_ref, l_ref, acc_ref):
    m_ref[...] = jnp.full(m_ref.shape, NEG, F32)
    l_ref[...] = jnp.zeros(l_ref.shape, F32)
    acc_ref[...] = jnp.zeros(acc_ref.shape, F32)


def _diff_combine(l_ref, acc_ref, t, lam):
    o = acc_ref[...] / jnp.sum(l_ref[...], axis=-1, keepdims=True)
    return o[:t] - lam * o[t:]


def _attn_prompt_kernel(q_ref, k_ref, v_ref, lam_ref, sg_ref, o_ref, m_ref, l_ref, acc_ref, *, tq, lam_init):
    qi = pl.program_id(2)
    qs = _stack_components(q_ref[0, 0])
    _init_online(m_ref, l_ref, acc_ref)

    def chunk(j, masked):
        start = pl.multiple_of(j * tq, tq)
        s = _dot_t(qs, k_ref[0, 0, pl.ds(start, tq), :])
        if masked:
            row = lax.broadcasted_iota(jnp.int32, s.shape, 0) % tq
            col = lax.broadcasted_iota(jnp.int32, s.shape, 1)
            s = jnp.where(col <= row, s, NEG)
        _online_update(s, v_ref[0, 0, pl.ds(start, tq), :], m_ref, l_ref, acc_ref)

    def body(j, carry):
        chunk(j, False)
        return carry

    lax.fori_loop(0, qi, body, 0)
    chunk(qi, True)
    o = _diff_combine(l_ref, acc_ref, tq, _lam(lam_ref, lam_init))
    o_ref[...] = _subln(o, sg_ref[...], lam_init).astype(BF16)


def _attn_prompt(q_bf, k_bf, v_bf, tq, lam_p, subln_g, lam_init):
    b, n_heads, t, _ = q_bf.shape
    nq = t // tq
    kern = functools.partial(_attn_prompt_kernel, tq=tq, lam_init=lam_init)
    seq = pl.BlockSpec((1, 1, t, HEAD_W), lambda bi, h, qi: (bi, h, 0, 0))
    return pl.pallas_call(
        kern,
        grid=(b, n_heads, nq),
        in_specs=[
            pl.BlockSpec((1, 1, tq, HEAD_W), lambda bi, h, qi: (bi, h, qi, 0)),
            seq,
            seq,
            pl.BlockSpec(lam_p.shape, lambda bi, h, qi: (0, 0)),
            pl.BlockSpec((1, HEAD_W), lambda bi, h, qi: (0, 0)),
        ],
        out_specs=pl.BlockSpec((tq, HEAD_W), lambda bi, h, qi: (bi * nq + qi, h)),
        out_shape=jax.ShapeDtypeStruct((b * t, n_heads * HEAD_W), BF16),
        scratch_shapes=[pltpu.VMEM((2 * tq, LANES), F32)] * 3,
        compiler_params=_cparams("arbitrary", "arbitrary", "arbitrary"),
        name="attn_prompt",
    )(q_bf, k_bf, v_bf, lam_p, subln_g.reshape(1, HEAD_W))


def _attn_sample_kernel(pt_ref, q_ref, kn_ref, vn_ref, *rest, ts, n_heads, pages, n_groups, lam_init):
    k_refs = rest[:pages]
    v_refs = rest[pages:2 * pages]
    lam_ref, sg_ref, o_ref, qs_ref, m_ref, l_ref, acc_ref = rest[2 * pages:]
    g = pl.program_id(1)

    @pl.when(g == 0)
    def _():
        for h in range(n_heads):
            qs_ref[h] = _stack_components(q_ref[0, h])
        _init_online(m_ref, l_ref, acc_ref)

    for h in range(n_heads):
        kh = jnp.concatenate([r[0, h].astype(BF16) for r in k_refs], axis=0)
        vh = jnp.concatenate([r[0, h].astype(BF16) for r in v_refs], axis=0)
        _online_update(_dot_t(qs_ref[h], kh), vh, m_ref.at[h], l_ref.at[h], acc_ref.at[h])

    @pl.when(g == n_groups - 1)
    def _():
        lam = _lam(lam_ref, lam_init)
        pad = jnp.zeros((LANES - ts, HEAD_W), BF16)
        for h in range(n_heads):
            kn = jnp.concatenate([kn_ref[0, h].astype(BF16), pad], axis=0)
            vn = jnp.concatenate([vn_ref[0, h].astype(BF16), pad], axis=0)
            s = _dot_t(qs_ref[h], kn)
            row = lax.broadcasted_iota(jnp.int32, s.shape, 0) % ts
            col = lax.broadcasted_iota(jnp.int32, s.shape, 1)
            s = jnp.where(col <= row, s, NEG)
            _online_update(s, vn, m_ref.at[h], l_ref.at[h], acc_ref.at[h])
            o = _diff_combine(l_ref.at[h], acc_ref.at[h], ts, lam)
            o_ref[:, h * HEAD_W:(h + 1) * HEAD_W] = _subln(o, sg_ref[...], lam_init).astype(BF16)


def _attn_sample(q_bf, k_new, v_new, cache_k, cache_v, page_table, pages, lam_p, subln_g, lam_init):
    bs, n_heads, ts, _ = q_bf.shape
    ps = cache_k.shape[2]
    n_groups = page_table.shape[1] // pages
    kern = functools.partial(_attn_sample_kernel, ts=ts, n_heads=n_heads, pages=pages, n_groups=n_groups,
                             lam_init=lam_init)

    def page_spec(i):
        return pl.BlockSpec((1, n_heads, ps, HEAD_W), lambda bi, g, pt: (pt[bi, g * pages + i], 0, 0, 0))

    new_rows = pl.BlockSpec((1, n_heads, ts, HEAD_W), lambda bi, g, pt: (bi, 0, 0, 0))
    grid_spec = pltpu.PrefetchScalarGridSpec(
        num_scalar_prefetch=1,
        grid=(bs, n_groups),
        in_specs=[new_rows] * 3
        + [page_spec(i) for i in range(pages)] * 2
        + [pl.BlockSpec(lam_p.shape, lambda bi, g, pt: (0, 0)), pl.BlockSpec((1, HEAD_W), lambda bi, g, pt: (0, 0))],
        out_specs=pl.BlockSpec((ts, n_heads * HEAD_W), lambda bi, g, pt: (bi, 0)),
        scratch_shapes=[pltpu.VMEM((n_heads, 2 * ts, HEAD_W), BF16)]
        + [pltpu.VMEM((n_heads, 2 * ts, LANES), F32)] * 3,
    )
    return pl.pallas_call(
        kern,
        grid_spec=grid_spec,
        out_shape=jax.ShapeDtypeStruct((bs * ts, n_heads * HEAD_W), BF16),
        compiler_params=_cparams("arbitrary", "arbitrary"),
        name="attn_sample",
    )(page_table, q_bf, k_new, v_new, *([cache_k] * pages), *([cache_v] * pages), lam_p, subln_g.reshape(1, HEAD_W))


def _mixb_kernel(mix_ref, qm_ref, x_ref, wout_ref, mk_ref, mv_ref, h_ref):
    c = mix_ref.shape[1]
    mo = _mem_attn(qm_ref[...], mk_ref[0], mv_ref[0])
    out = _dot(mix_ref[...], wout_ref[:c, :]) + _dot(mo.astype(BF16), wout_ref[c:, :])
    h_ref[...] = x_ref[...] + out


def _mixb(mix_bf, qm, x2d, b, t, tm, wout_bf, mem_k, mem_v):
    m, d = x2d.shape
    c = mix_bf.shape[1]
    n_t = t // tm
    n_mem = mem_k.shape[2]
    row = lambda bi, i: (bi * n_t + i, 0)
    return pl.pallas_call(
        _mixb_kernel,
        grid=(b, n_t),
        in_specs=[
            pl.BlockSpec((tm, c), row),
            pl.BlockSpec((tm, MEM_W), row),
            pl.BlockSpec((tm, d), row),
            _resident(wout_bf.shape),
            _mem_spec(n_mem),
            _mem_spec(n_mem),
        ],
        out_specs=pl.BlockSpec((tm, d), row),
        out_shape=jax.ShapeDtypeStruct((m, d), F32),
        compiler_params=_cparams("arbitrary", "arbitrary"),
        name="mixb",
    )(mix_bf, qm, x2d, wout_bf, mem_k, mem_v)


def _router_kernel(h_ref, g_ref, r_ref, o_ref, *, n_experts):
    hn = _rms(h_ref[...], g_ref[...])
    logits = jnp.dot(hn, r_ref[...], preferred_element_type=F32, precision=lax.Precision.HIGHEST)
    lane = lax.broadcasted_iota(jnp.int32, logits.shape, 1)
    logits = jnp.where(lane < n_experts, logits, -jnp.inf)
    v1 = jnp.max(logits, axis=-1, keepdims=True)
    e1 = jnp.min(jnp.where(logits == v1, lane, LANES), axis=-1, keepdims=True)
    rest = jnp.where(lane == e1, -jnp.inf, logits)
    v2 = jnp.max(rest, axis=-1, keepdims=True)
    e2 = jnp.min(jnp.where(rest == v2, lane, LANES), axis=-1, keepdims=True)
    ex = jnp.exp(v2 - v1)
    g1 = 1.0 / (1.0 + ex)
    g2 = ex / (1.0 + ex)
    out = jnp.where(lane == 0, e1.astype(F32), 0.0)
    out = jnp.where(lane == 1, e2.astype(F32), out)
    out = jnp.where(lane == 2, g1, out)
    out = jnp.where(lane == 3, g2, out)
    o_ref[...] = out


def _router(h2d, tm, g, router):
    m, d = h2d.shape
    n_experts = router.shape[1]
    r_pad = jnp.zeros((d, LANES), F32).at[:, :n_experts].set(router)
    kern = functools.partial(_router_kernel, n_experts=n_experts)
    return pl.pallas_call(
        kern,
        grid=(m // tm,),
        in_specs=[pl.BlockSpec((tm, d), lambda i: (i, 0)), _resident((1, d)), _resident((d, LANES))],
        out_specs=pl.BlockSpec((tm, LANES), lambda i: (i, 0)),
        out_shape=jax.ShapeDtypeStruct((m, LANES), F32),
        compiler_params=_cparams("arbitrary"),
        name="router",
    )(h2d, g.reshape(1, d), r_pad)


GATHER_UNROLL = 8


def _row_copy(src_hbm, row, dst, slot, sem):
    return pltpu.make_async_copy(src_hbm.at[pl.ds(row, 1)], dst.at[pl.ds(slot, 1)], sem)


def _gather_start(src_hbm, idx_ref, base, dst, sem, n):
    def issue(r, carry):
        _row_copy(src_hbm, idx_ref[base + r], dst, r, sem).start()
        return carry

    lax.fori_loop(0, n, issue, 0, unroll=GATHER_UNROLL)


def _gather_wait(src_hbm, dst, sem, n):
    def wait(r, carry):
        _row_copy(src_hbm, 0, dst, r, sem).wait()
        return carry

    lax.fori_loop(0, n, wait, 0, unroll=GATHER_UNROLL)


def _expert_kernel(blk_e_ref, tok_ref, nused_ref, h_hbm, g_ref, wg_ref, wu_ref, wd_ref, o_ref, xbuf, sem, *, blk):
    i = pl.program_id(0)
    n_used = nused_ref[0]
    slot = i % 2

    @pl.when(i == 0)
    def _():
        _gather_start(h_hbm, tok_ref, 0, xbuf.at[0], sem.at[0], blk)

    @pl.when(i + 1 < n_used)
    def _():
        _gather_start(h_hbm, tok_ref, (i + 1) * blk, xbuf.at[1 - slot], sem.at[1 - slot], blk)

    @pl.when(i < n_used)
    def _():
        _gather_wait(h_hbm, xbuf.at[slot], sem.at[slot], blk)
        hn = _rms(xbuf[slot], g_ref[...]).astype(BF16)
        a = _silu(_dot(hn, wg_ref[0])) * _dot(hn, wu_ref[0])
        o_ref[...] = _dot(a.astype(BF16), wd_ref[0])

    @pl.when(i >= n_used)
    def _():
        o_ref[...] = jnp.zeros(o_ref.shape, F32)


def _experts(h2d, g, wg_bf, wu_bf, wd_bf, blk_e, buf_tok, n_used, blk):
    m, d = h2d.shape
    n_rows = buf_tok.shape[0]
    n_blk = n_rows // blk
    ff = wg_bf.shape[2]
    kern = functools.partial(_expert_kernel, blk=blk)
    grid_spec = pltpu.PrefetchScalarGridSpec(
        num_scalar_prefetch=3,
        grid=(n_blk,),
        in_specs=[
            pl.BlockSpec(memory_space=pl.ANY),
            pl.BlockSpec((1, d), lambda i, be, tk, nu: (0, 0)),
            pl.BlockSpec((1, d, ff), lambda i, be, tk, nu: (be[i], 0, 0)),
            pl.BlockSpec((1, d, ff), lambda i, be, tk, nu: (be[i], 0, 0)),
            pl.BlockSpec((1, ff, d), lambda i, be, tk, nu: (be[i], 0, 0)),
        ],
        out_specs=pl.BlockSpec((blk, d), lambda i, be, tk, nu: (i, 0)),
        scratch_shapes=[pltpu.VMEM((2, blk, d), F32), pltpu.SemaphoreType.DMA((2,))],
    )
    return pl.pallas_call(
        kern,
        grid_spec=grid_spec,
        out_shape=jax.ShapeDtypeStruct((n_rows, d), F32),
        compiler_params=_cparams("arbitrary"),
        name="experts",
    )(blk_e, buf_tok, n_used, h2d, g.reshape(1, d), wg_bf, wu_bf, wd_bf)


def _combine_kernel(pos_ref, h_ref, route_ref, g_ref, yb_hbm, o_ref, ybuf, sem, *, tc, n_steps):
    i = pl.program_id(0)
    slot = i % 2

    def start(step, s):
        for k in range(TOP_K):
            _gather_start(yb_hbm, pos_ref, (k * n_steps + step) * tc, ybuf.at[s, k], sem.at[s], tc)

    @pl.when(i == 0)
    def _():
        start(0, 0)

    @pl.when(i + 1 < n_steps)
    def _():
        start(i + 1, 1 - slot)

    for k in range(TOP_K):
        _gather_wait(yb_hbm, ybuf.at[slot, k], sem.at[slot], tc)
    route = route_ref[...]
    y = ybuf[slot, 0] * route[:, 2:3] + ybuf[slot, 1] * route[:, 3:4]
    o_ref[...] = _rms(h_ref[...] + y, g_ref[...])


def _combine(h2d, route, yb, pos_k_major, g, tc):
    m, d = h2d.shape
    n_steps = m // tc
    kern = functools.partial(_combine_kernel, tc=tc, n_steps=n_steps)
    grid_spec = pltpu.PrefetchScalarGridSpec(
        num_scalar_prefetch=1,
        grid=(n_steps,),
        in_specs=[
            pl.BlockSpec((tc, d), lambda i, p: (i, 0)),
            pl.BlockSpec((tc, LANES), lambda i, p: (i, 0)),
            pl.BlockSpec((1, d), lambda i, p: (0, 0)),
            pl.BlockSpec(memory_space=pl.ANY),
        ],
        out_specs=pl.BlockSpec((tc, d), lambda i, p: (i, 0)),
        scratch_shapes=[pltpu.VMEM((2, TOP_K, tc, d), F32), pltpu.SemaphoreType.DMA((2,))],
    )
    return pl.pallas_call(
        kern,
        grid_spec=grid_spec,
        out_shape=jax.ShapeDtypeStruct((m, d), F32),
        compiler_params=_cparams("arbitrary"),
        name="combine",
    )(pos_k_major, h2d, route, g.reshape(1, d), yb)


def _moe(h2d, tm, g_ffn, router, wg_bf, wu_bf, wd_bf, g_final, blk):
    m, d = h2d.shape
    n_experts = router.shape[1]
    route = _router(h2d, tm, g_ffn, router)
    flat_e = route[:, :TOP_K].astype(jnp.int32).reshape(-1)
    a = flat_e.shape[0]
    onehot = (flat_e[:, None] == jnp.arange(n_experts, dtype=jnp.int32)[None, :]).astype(jnp.int32)
    csum = jnp.cumsum(onehot, axis=0)
    counts = csum[-1]
    rank = jnp.take_along_axis(csum, flat_e[:, None], axis=1)[:, 0] - 1
    padded = (counts + blk - 1) // blk * blk
    cum_pad = jnp.cumsum(padded)
    dest = (cum_pad - padded)[flat_e] + rank
    n_blk = -(-(a + n_experts * (blk - 1)) // blk)
    n_rows = n_blk * blk
    buf_tok = jnp.zeros((n_rows,), jnp.int32).at[dest].set(jnp.arange(a, dtype=jnp.int32) // TOP_K)
    n_used = (cum_pad[-1] // blk).astype(jnp.int32)
    blk_i = jnp.minimum(jnp.arange(n_blk, dtype=jnp.int32), n_used - 1)
    blk_e = jnp.minimum(jnp.searchsorted(cum_pad, blk_i * blk, side='right'), n_experts - 1).astype(jnp.int32)
    yb = _experts(h2d, g_ffn, wg_bf, wu_bf, wd_bf, blk_e, buf_tok, n_used.reshape(1), blk)
    return _combine(h2d, route, yb, dest.reshape(m, TOP_K).T.reshape(-1), g_final, tm)


def _trunk(x, pos, conv_prev, mem_k, mem_v, attend, w, tm, moe_tm, moe_blk):
    b, t, d = x.shape
    x2d = x.reshape(b * t, d)
    h, conv_st = _mixa(x2d, b, t, tm, w['g_mix'][0], w['a_w_in'], w['a_conv'], w['a_w_out'], mem_k[0], mem_v[0],
                       conv_prev)
    h = _ffn(h, moe_tm, w['g_ffn'][0], w['f_w_gate'], w['f_w_up'], w['f_w_down'])
    cos, sin = _rope_tables(pos)
    k, v, k_bf, v_bf = _kvproj(h, b, t, tm, w['g_kv'], w['w_kv'], cos, sin)
    q_bf, qm = _qproj(h, b, t, tm, w['g_mix'][1], w['b_w_in'], cos, sin)
    mix = attend(q_bf, k, v, k_bf, v_bf)
    h = _mixb(mix, qm, h, b, t, tm, w['b_w_out'], mem_k[1], mem_v[1])
    y = _moe(h, moe_tm, w['g_ffn'][1], w['m_router'], w['m_w_gate'], w['m_w_up'], w['m_w_down'], w['g_final'],
             moe_blk)
    return y.reshape(b, t, d), conv_st, k, v


def _token_major(x):
    return jnp.transpose(x, (0, 2, 1, 3))


def kernel(x_prompt, x_sample, state_conv, cache_k, cache_v, cache_mem_k, cache_mem_v, page_table, mem_prompt, g_mix, g_ffn, g_mem, w_mem_kv, a_w_in, a_conv, a_w_out, g_kv, w_kv, b_w_in, b_lambda, b_subln, b_w_out, f_w_gate, f_w_up, f_w_down, m_router, m_w_gate, m_w_up, m_w_down, g_final):
    bp, tp, d = x_prompt.shape
    bs, ts, _ = x_sample.shape
    depth = g_mix.shape[0]
    n_a = a_w_in.shape[0]
    assert depth == 2 and n_a == 1 and b_w_in.shape[0] == 1, "one conv layer followed by one attention layer"
    assert ts == SUBLANES and tp % 512 == 0
    c = d - MEM_W
    lam_init = 0.8 - 0.6 * math.exp(-0.3 * n_a)
    w = {
        'g_mix': g_mix, 'g_ffn': g_ffn, 'g_kv': g_kv, 'g_final': g_final,
        'a_w_in': a_w_in[0].astype(BF16), 'a_conv': a_conv[0], 'a_w_out': a_w_out[0].astype(BF16),
        'w_kv': w_kv.astype(BF16), 'b_w_in': b_w_in[0].astype(BF16), 'b_w_out': b_w_out[0].astype(BF16),
        'f_w_gate': f_w_gate[0].astype(BF16), 'f_w_up': f_w_up[0].astype(BF16), 'f_w_down': f_w_down[0].astype(BF16),
        'm_router': m_router[0], 'm_w_gate': m_w_gate[0].astype(BF16), 'm_w_up': m_w_up[0].astype(BF16),
        'm_w_down': m_w_down[0].astype(BF16),
    }
    lam_p = b_lambda[0]
    subln_g = b_subln[0]

    n_mem = mem_prompt.shape[1]
    mem_kt, mem_vt = _memkv(mem_prompt, g_mem, w_mem_kv.astype(BF16))

    def mem_out(x):
        return jnp.transpose(x.reshape(depth, bp, MEM_HEADS, MEM_HEAD_DIM, n_mem), (0, 1, 4, 2, 3))

    def attend_prompt(q_bf, k, v, k_bf, v_bf):
        return _attn_prompt(q_bf, k_bf, v_bf, 512, lam_p, subln_g, lam_init)

    y_p, conv_p, k_p, v_p = _trunk(
        x_prompt, jnp.arange(tp), jnp.zeros((bp, CONV_WIDTH - 1, c), F32), mem_kt, mem_vt, attend_prompt, w,
        tm=256, moe_tm=256, moe_blk=256)

    past = page_table.shape[1] * cache_k.shape[1]
    pages = math.gcd(page_table.shape[1], 16)
    cache_kh = jnp.transpose(cache_k, (0, 2, 1, 3))
    cache_vh = jnp.transpose(cache_v, (0, 2, 1, 3))

    def mem_in(x):
        return jnp.transpose(x, (0, 1, 3, 4, 2)).reshape(depth, bs, MEM_W, x.shape[2])

    def attend_sample(q_bf, k, v, k_bf, v_bf):
        return _attn_sample(q_bf, k, v, cache_kh, cache_vh, page_table, pages, lam_p, subln_g, lam_init)

    y_s, conv_s, k_s, v_s = _trunk(
        x_sample, past + jnp.arange(ts), state_conv[0], mem_in(cache_mem_k), mem_in(cache_mem_v), attend_sample, w,
        tm=ts, moe_tm=bs * ts, moe_blk=128)

    return (y_p, y_s, conv_p[None], conv_s[None], _token_major(k_p), _token_major(v_p), _token_major(k_s),
            _token_major(v_s), mem_out(mem_kt), mem_out(mem_vt))
```

```python
import functools
import math

import jax
import jax.numpy as jnp
from jax import lax
from jax.experimental import pallas as pl
from jax.experimental.pallas import tpu as pltpu

F32 = jnp.float32
BF16 = jnp.bfloat16

EPS = 1e-6
MEM_HEADS = 4
MEM_HEAD_DIM = 64
MEM_W = MEM_HEADS * MEM_HEAD_DIM
HEAD_W = 128
HALF_W = HEAD_W // 2
ROPE_HALF = HALF_W // 2
ROPE_THETA = 10000.0
CONV_WIDTH = 3
TOP_K = 2
NEG = -1e30
QK_SCALE = HALF_W ** -0.5 * math.log2(math.e)
MEM_SCALE = MEM_HEAD_DIM ** -0.5
LANES = 128
SUBLANES = 8
VMEM_LIMIT = 56 * 1024 * 1024


def _cparams(*sem):
    return pltpu.CompilerParams(dimension_semantics=sem, vmem_limit_bytes=VMEM_LIMIT)


def _rms(x, g):
    return x * lax.rsqrt(jnp.mean(x * x, axis=-1, keepdims=True) + EPS) * g


def _dot(a, b):
    return jnp.dot(a, b, preferred_element_type=F32)


def _dot_t(a, b):
    return lax.dot_general(a, b, (((1,), (1,)), ((), ())), preferred_element_type=F32)


def _resident(shape):
    n = len(shape)
    return pl.BlockSpec(shape, lambda *_: (0,) * n, pipeline_mode=pl.Buffered(1))


def _tile_rows(j, n):
    return pl.ds(j, n, stride=SUBLANES)


def _mem_attn(qm, mk_t, mv_t):
    q = (qm * MEM_SCALE).astype(BF16)
    kb = mk_t.astype(BF16)
    vb = mv_t.astype(BF16)
    lane = lax.broadcasted_iota(jnp.int32, q.shape, 1)
    out = jnp.zeros(q.shape, F32)
    for h in range(MEM_HEADS):
        in_head = (lane >= h * MEM_HEAD_DIM) & (lane < (h + 1) * MEM_HEAD_DIM)
        s = _dot(jnp.where(in_head, q, jnp.zeros_like(q)), kb)
        m = jnp.max(s, axis=-1, keepdims=True)
        p = jnp.exp(s - m)
        p = p / jnp.sum(p, axis=-1, keepdims=True)
        out = jnp.where(in_head, _dot_t(p.astype(BF16), vb), out)
    return out


def _rope_slab(x, cos, sin_signed):
    lane = lax.broadcasted_iota(jnp.int32, x.shape, 1)
    first_half = (lane % HALF_W) < ROPE_HALF
    partner = jnp.where(first_half, pltpu.roll(x, HEAD_W - ROPE_HALF, 1), pltpu.roll(x, ROPE_HALF, 1))
    return x * cos + partner * sin_signed


def _lam(lam_ref, lam_init):
    lp = lam_ref[...]
    a = jnp.sum(lp[0:1] * lp[1:2], axis=-1, keepdims=True)
    b = jnp.sum(lp[2:3] * lp[3:4], axis=-1, keepdims=True)
    return jnp.exp(a) - jnp.exp(b) + lam_init


def _subln(o, g, lam_init):
    o = o * lax.rsqrt(jnp.mean(o * o, axis=-1, keepdims=True) + EPS)
    return o * g * (1.0 - lam_init)


def _memkv_kernel(x_ref, g_ref, w_ref, k_ref, v_ref):
    hn = _rms(x_ref[0], g_ref[0]).astype(BF16)
    kv_t = _dot(hn, w_ref[0]).T
    k_ref[0, 0] = kv_t[:MEM_W]
    v_ref[0, 0] = kv_t[MEM_W:]


def _memkv(mem, g_mem, w_bf):
    depth, d, n = w_bf.shape
    b, n_mem, _ = mem.shape
    out = jax.ShapeDtypeStruct((depth, b, MEM_W, n_mem), F32)
    return pl.pallas_call(
        _memkv_kernel,
        grid=(depth, b),
        in_specs=[
            pl.BlockSpec((1, n_mem, d), lambda l, bi: (bi, 0, 0)),
            pl.BlockSpec((1, 1, d), lambda l, bi: (l, 0, 0)),
            pl.BlockSpec((1, d, n), lambda l, bi: (l, 0, 0)),
        ],
        out_specs=[pl.BlockSpec((1, 1, MEM_W, n_mem), lambda l, bi: (l, bi, 0, 0))] * 2,
        out_shape=[out, out],
        compiler_params=_cparams("arbitrary", "arbitrary"),
        name="memkv",
    )(mem, g_mem.reshape(depth, 1, d), w_bf)


def _mixa_kernel(x_ref, g_ref, win_ref, cw_ref, wout_ref, mk_ref, mv_ref, prev_ref, h_ref, st_ref, ubuf,
                 *, tm, c, n_t):
    i = pl.program_id(1)
    x = x_ref[...]
    hn = _rms(x, g_ref[...]).astype(BF16)
    proj = _dot(hn, win_ref[...])
    gate_b = proj[:, :c]
    u = proj[:, c:2 * c] * proj[:, 2 * c:3 * c]
    qm = proj[:, 3 * c:]

    @pl.when(i == 0)
    def _():
        ubuf[SUBLANES - 2:SUBLANES, :] = prev_ref[0]

    @pl.when(i > 0)
    def _():
        ubuf[SUBLANES - 2:SUBLANES, :] = ubuf[tm + SUBLANES - 2:tm + SUBLANES, :]

    ubuf[SUBLANES:SUBLANES + tm, :] = u
    cw = cw_ref[...]
    y = cw[0:1] * ubuf[SUBLANES - 2:SUBLANES - 2 + tm, :]
    y = y + cw[1:2] * ubuf[SUBLANES - 1:SUBLANES - 1 + tm, :]
    y = y + cw[2:3] * u
    mix = gate_b * y
    mo = _mem_attn(qm, mk_ref[0], mv_ref[0])
    out = _dot(mix.astype(BF16), wout_ref[:c, :]) + _dot(mo.astype(BF16), wout_ref[c:, :])
    h_ref[...] = x + out

    @pl.when(i == n_t - 1)
    def _():
        st_ref[0] = ubuf[tm + SUBLANES - 2:tm + SUBLANES, :]


def _mem_spec(n_mem):
    return pl.BlockSpec((1, MEM_W, n_mem), lambda bi, i: (bi, 0, 0))


def _mixa(x2d, b, t, tm, g, win_bf, conv_w, wout_bf, mem_k, mem_v, prev):
    d = x2d.shape[1]
    c = d - MEM_W
    n_t = t // tm
    n_mem = mem_k.shape[2]
    kern = functools.partial(_mixa_kernel, tm=tm, c=c, n_t=n_t)
    return pl.pallas_call(
        kern,
        grid=(b, n_t),
        in_specs=[
            pl.BlockSpec((tm, d), lambda bi, i: (bi * n_t + i, 0)),
            _resident((1, d)),
            _resident(win_bf.shape),
            _resident(conv_w.shape),
            _resident(wout_bf.shape),
            _mem_spec(n_mem),
            _mem_spec(n_mem),
            pl.BlockSpec((1, CONV_WIDTH - 1, c), lambda bi, i: (bi, 0, 0)),
        ],
        out_specs=[
            pl.BlockSpec((tm, d), lambda bi, i: (bi * n_t + i, 0)),
            pl.BlockSpec((1, CONV_WIDTH - 1, c), lambda bi, i: (bi, 0, 0)),
        ],
        out_shape=[
            jax.ShapeDtypeStruct(x2d.shape, F32),
            jax.ShapeDtypeStruct((b, CONV_WIDTH - 1, c), F32),
        ],
        scratch_shapes=[pltpu.VMEM((tm + SUBLANES, c), F32)],
        compiler_params=_cparams("arbitrary", "arbitrary"),
        name="mixa",
    )(x2d, g.reshape(1, d), win_bf, conv_w, wout_bf, mem_k, mem_v, prev)


def _silu(g):
    return g / (1.0 + jnp.exp(-g))


def _ffn_kernel(h_ref, g_ref, wg_ref, wu_ref, wd_ref, o_ref):
    h = h_ref[...]
    hn = _rms(h, g_ref[...]).astype(BF16)
    a = _silu(_dot(hn, wg_ref[...])) * _dot(hn, wu_ref[...])
    o_ref[...] = h + _dot(a.astype(BF16), wd_ref[...])


def _ffn(h2d, tm, g, wg_bf, wu_bf, wd_bf):
    m, d = h2d.shape
    return pl.pallas_call(
        _ffn_kernel,
        grid=(m // tm,),
        in_specs=[
            pl.BlockSpec((tm, d), lambda i: (i, 0)),
            _resident((1, d)),
            _resident(wg_bf.shape),
            _resident(wu_bf.shape),
            _resident(wd_bf.shape),
        ],
        out_specs=pl.BlockSpec((tm, d), lambda i: (i, 0)),
        out_shape=jax.ShapeDtypeStruct(h2d.shape, F32),
        compiler_params=_cparams("arbitrary"),
        name="ffn",
    )(h2d, g.reshape(1, d), wg_bf, wu_bf, wd_bf)


def _kv_kernel(h_ref, g_ref, w_ref, cos_ref, sin_ref, k_ref, v_ref, kb_ref, vb_ref, *, n_heads):
    hn = _rms(h_ref[...], g_ref[...]).astype(BF16)
    kv = _dot(hn, w_ref[...])
    cos = cos_ref[...]
    sin = sin_ref[...]
    for h in range(n_heads):
        r = _rope_slab(kv[:, h * HEAD_W:(h + 1) * HEAD_W], cos, sin)
        k_ref[0, h] = r
        kb_ref[0, h] = r.astype(BF16)
        v = kv[:, (n_heads + h) * HEAD_W:(n_heads + h + 1) * HEAD_W]
        v_ref[0, h] = v
        vb_ref[0, h] = v.astype(BF16)


def _q_kernel(h_ref, g_ref, w_ref, cos_ref, sin_ref, q_ref, qm_ref, *, n_heads):
    hn = _rms(h_ref[...], g_ref[...]).astype(BF16)
    p = _dot(hn, w_ref[...])
    cos = cos_ref[...]
    sin = sin_ref[...]
    for h in range(n_heads):
        q_ref[0, h] = (_rope_slab(p[:, h * HEAD_W:(h + 1) * HEAD_W], cos, sin) * QK_SCALE).astype(BF16)
    qm_ref[...] = p[:, n_heads * HEAD_W:]


def _rope_tables(pos):
    inv = ROPE_THETA ** (-jnp.arange(ROPE_HALF, dtype=F32) / ROPE_HALF)
    ang = pos.astype(F32)[:, None] * inv[None, :]
    cos = jnp.tile(jnp.cos(ang), (1, HEAD_W // ROPE_HALF))
    sin = jnp.sin(ang)
    sin = jnp.tile(jnp.concatenate([-sin, sin], axis=-1), (1, HEAD_W // HALF_W))
    return cos, sin


def _kvproj(h2d, b, t, tm, g, w_bf, cos, sin):
    m, d = h2d.shape
    n_heads = w_bf.shape[1] // (2 * HEAD_W)
    n_t = t // tm
    kern = functools.partial(_kv_kernel, n_heads=n_heads)
    head_major = pl.BlockSpec((1, n_heads, tm, HEAD_W), lambda bi, i: (bi, 0, i, 0))
    rope = pl.BlockSpec((tm, HEAD_W), lambda bi, i: (i, 0))
    return pl.pallas_call(
        kern,
        grid=(b, n_t),
        in_specs=[pl.BlockSpec((tm, d), lambda bi, i: (bi * n_t + i, 0)), _resident((1, d)), _resident(w_bf.shape),
                  rope, rope],
        out_specs=[head_major] * 4,
        out_shape=[jax.ShapeDtypeStruct((b, n_heads, t, HEAD_W), F32)] * 2
        + [jax.ShapeDtypeStruct((b, n_heads, t, HEAD_W), BF16)] * 2,
        compiler_params=_cparams("arbitrary", "arbitrary"),
        name="kvproj",
    )(h2d, g.reshape(1, d), w_bf, cos, sin)


def _qproj(h2d, b, t, tm, g, w_bf, cos, sin):
    m, d = h2d.shape
    n_heads = (w_bf.shape[1] - MEM_W) // HEAD_W
    n_t = t // tm
    kern = functools.partial(_q_kernel, n_heads=n_heads)
    rope = pl.BlockSpec((tm, HEAD_W), lambda bi, i: (i, 0))
    return pl.pallas_call(
        kern,
        grid=(b, n_t),
        in_specs=[pl.BlockSpec((tm, d), lambda bi, i: (bi * n_t + i, 0)), _resident((1, d)), _resident(w_bf.shape),
                  rope, rope],
        out_specs=[pl.BlockSpec((1, n_heads, tm, HEAD_W), lambda bi, i: (bi, 0, i, 0)),
                   pl.BlockSpec((tm, MEM_W), lambda bi, i: (bi * n_t + i, 0))],
        out_shape=[jax.ShapeDtypeStruct((b, n_heads, t, HEAD_W), BF16), jax.ShapeDtypeStruct((m, MEM_W), F32)],
        compiler_params=_cparams("arbitrary", "arbitrary"),
        name="qproj",
    )(h2d, g.reshape(1, d), w_bf, cos, sin)


def _stack_components(q):
    lane = lax.broadcasted_iota(jnp.int32, q.shape, 1)
    zero = jnp.zeros_like(q)
    return jnp.concatenate([jnp.where(lane < HALF_W, q, zero), jnp.where(lane >= HALF_W, q, zero)], axis=0)


def _online_update(s, v_bf, m_ref, l_ref, acc_ref):
    cols = [s[:, c * LANES:(c + 1) * LANES] for c in range(s.shape[1] // LANES)]
    m_prev = m_ref[...]
    m_new = jnp.maximum(m_prev, jnp.max(functools.reduce(jnp.maximum, cols), axis=-1, keepdims=True))
    alpha = jnp.exp2(m_prev - m_new)
    ps = [jnp.exp2(c - m_new) for c in cols]
    l_ref[...] = alpha * l_ref[...] + functools.reduce(jnp.add, ps)
    p = jnp.concatenate(ps, axis=1).astype(BF16)
    acc_ref[...] = alpha * acc_ref[...] + _dot(p, v_bf)
    m_ref[...] = m_new


def _init_online(m_ref, l_ref, acc_ref):
    m_ref[...] = jnp.full(m_ref.shape, NEG, F32)
    l_ref[...] = jnp.zeros(l_ref.shape, F32)
    acc_ref[...] = jnp.zeros(acc_ref.shape, F32)


def _diff_combine(l_ref, acc_ref, t, lam):
    o = acc_ref[...] / jnp.sum(l_ref[...], axis=-1, keepdims=True)
    return o[:t] - lam * o[t:]


def _attn_prompt_kernel(q_ref, k_ref, v_ref, lam_ref, sg_ref, o_ref, m_ref, l_ref, acc_ref, *, tq, lam_init):
    qi = pl.program_id(2)
    qs = _stack_components(q_ref[0, 0])
    _init_online(m_ref, l_ref, acc_ref)

    def chunk(j, masked):
        start = pl.multiple_of(j * tq, tq)
        s = _dot_t(qs, k_ref[0, 0, pl.ds(start, tq), :])
        if masked:
            row = lax.broadcasted_iota(jnp.int32, s.shape, 0) % tq
            col = lax.broadcasted_iota(jnp.int32, s.shape, 1)
            s = jnp.where(col <= row, s, NEG)
        _online_update(s, v_ref[0, 0, pl.ds(start, tq), :], m_ref, l_ref, acc_ref)

    def body(j, carry):
        chunk(j, False)
        return carry

    lax.fori_loop(0, qi, body, 0)
    chunk(qi, True)
    o = _diff_combine(l_ref, acc_ref, tq, _lam(lam_ref, lam_init))
    o_ref[...] = _subln(o, sg_ref[...], lam_init).astype(BF16)


def _attn_prompt(q_bf, k_bf, v_bf, tq, lam_p, subln_g, lam_init):
    b, n_heads, t, _ = q_bf.shape
    nq = t // tq
    kern = functools.partial(_attn_prompt_kernel, tq=tq, lam_init=lam_init)
    seq = pl.BlockSpec((1, 1, t, HEAD_W), lambda bi, h, qi: (bi, h, 0, 0))
    return pl.pallas_call(
        kern,
        grid=(b, n_heads, nq),
        in_specs=[
            pl.BlockSpec((1, 1, tq, HEAD_W), lambda bi, h, qi: (bi, h, qi, 0)),
            seq,
            seq,
            pl.BlockSpec(lam_p.shape, lambda bi, h, qi: (0, 0)),
            pl.BlockSpec((1, HEAD_W), lambda bi, h, qi: (0, 0)),
        ],
        out_specs=pl.BlockSpec((tq, HEAD_W), lambda bi, h, qi: (bi * nq + qi, h)),
        out_shape=jax.ShapeDtypeStruct((b * t, n_heads * HEAD_W), BF16),
        scratch_shapes=[pltpu.VMEM((2 * tq, LANES), F32)] * 3,
        compiler_params=_cparams("arbitrary", "arbitrary", "arbitrary"),
        name="attn_prompt",
    )(q_bf, k_bf, v_bf, lam_p, subln_g.reshape(1, HEAD_W))


def _attn_sample_kernel(pt_ref, q_ref, kn_ref, vn_ref, *rest, ts, n_heads, pages, n_groups, lam_init):
    k_refs = rest[:pages]
    v_refs = rest[pages:2 * pages]
    lam_ref, sg_ref, o_ref, qs_ref, m_ref, l_ref, acc_ref = rest[2 * pages:]
    g = pl.program_id(1)

    @pl.when(g == 0)
    def _():
        for h in range(n_heads):
            qs_ref[h] = _stack_components(q_ref[0, h])
        _init_online(m_ref, l_ref, acc_ref)

    for h in range(n_heads):
        kh = jnp.concatenate([r[0, h].astype(BF16) for r in k_refs], axis=0)
        vh = jnp.concatenate([r[0, h].astype(BF16) for r in v_refs], axis=0)
        _online_update(_dot_t(qs_ref[h], kh), vh, m_ref.at[h], l_ref.at[h], acc_ref.at[h])

    @pl.when(g == n_groups - 1)
    def _():
        lam = _lam(lam_ref, lam_init)
        pad = jnp.zeros((LANES - ts, HEAD_W), BF16)
        for h in range(n_heads):
            kn = jnp.concatenate([kn_ref[0, h].astype(BF16), pad], axis=0)
            vn = jnp.concatenate([vn_ref[0, h].astype(BF16), pad], axis=0)
            s = _dot_t(qs_ref[h], kn)
            row = lax.broadcasted_iota(jnp.int32, s.shape, 0) % ts
            col = lax.broadcasted_iota(jnp.int32, s.shape, 1)
            s = jnp.where(col <= row, s, NEG)
            _online_update(s, vn, m_ref.at[h], l_ref.at[h], acc_ref.at[h])
            o = _diff_combine(l_ref.at[h], acc_ref.at[h], ts, lam)
            o_ref[:, h * HEAD_W:(h + 1) * HEAD_W] = _subln(o, sg_ref[...], lam_init).astype(BF16)


def _attn_sample(q_bf, k_new, v_new, cache_k, cache_v, page_table, pages, lam_p, subln_g, lam_init):
    bs, n_heads, ts, _ = q_bf.shape
    ps = cache_k.shape[2]
    n_groups = page_table.shape[1] // pages
    kern = functools.partial(_attn_sample_kernel, ts=ts, n_heads=n_heads, pages=pages, n_groups=n_groups,
                             lam_init=lam_init)

    def page_spec(i):
        return pl.BlockSpec((1, n_heads, ps, HEAD_W), lambda bi, g, pt: (pt[bi, g * pages + i], 0, 0, 0))

    new_rows = pl.BlockSpec((1, n_heads, ts, HEAD_W), lambda bi, g, pt: (bi, 0, 0, 0))
    grid_spec = pltpu.PrefetchScalarGridSpec(
        num_scalar_prefetch=1,
        grid=(bs, n_groups),
        in_specs=[new_rows] * 3
        + [page_spec(i) for i in range(pages)] * 2
        + [pl.BlockSpec(lam_p.shape, lambda bi, g, pt: (0, 0)), pl.BlockSpec((1, HEAD_W), lambda bi, g, pt: (0, 0))],
        out_specs=pl.BlockSpec((ts, n_heads * HEAD_W), lambda bi, g, pt: (bi, 0)),
        scratch_shapes=[pltpu.VMEM((n_heads, 2 * ts, HEAD_W), BF16)]
        + [pltpu.VMEM((n_heads, 2 * ts, LANES), F32)] * 3,
    )
    return pl.pallas_call(
        kern,
        grid_spec=grid_spec,
        out_shape=jax.ShapeDtypeStruct((bs * ts, n_heads * HEAD_W), BF16),
        compiler_params=_cparams("arbitrary", "arbitrary"),
        name="attn_sample",
    )(page_table, q_bf, k_new, v_new, *([cache_k] * pages), *([cache_v] * pages), lam_p, subln_g.reshape(1, HEAD_W))


def _route(h, g, r_hi, r_lo, base, n_experts):
    hn = _rms(h, g)
    hi = hn.astype(BF16)
    lo = (hn - hi.astype(F32)).astype(BF16)
    logits = _dot(hi, r_hi) + (_dot(hi, r_lo) + _dot(lo, r_hi))
    lane = lax.broadcasted_iota(jnp.int32, logits.shape, 1)
    logits = jnp.where(lane < n_experts, logits, -jnp.inf)
    v1 = jnp.max(logits, axis=-1, keepdims=True)
    e1 = jnp.min(jnp.where(logits == v1, lane, LANES), axis=-1, keepdims=True)
    rest = jnp.where(lane == e1, -jnp.inf, logits)
    v2 = jnp.max(rest, axis=-1, keepdims=True)
    e2 = jnp.min(jnp.where(rest == v2, lane, LANES), axis=-1, keepdims=True)
    ex = jnp.exp(v2 - v1)
    g1 = 1.0 / (1.0 + ex)
    g2 = ex / (1.0 + ex)
    sel = jnp.where((lane == e1) | (lane == e2), 1.0, 0.0)
    t = h.shape[0]
    earlier = lax.broadcasted_iota(jnp.int32, (t, t), 0) > lax.broadcasted_iota(jnp.int32, (t, t), 1)
    cum = _dot(jnp.where(earlier, 1.0, 0.0).astype(BF16), sel.astype(BF16)) + base
    r1 = jnp.sum(jnp.where(lane == e1, cum, 0.0), axis=-1, keepdims=True)
    r2 = jnp.sum(jnp.where(lane == e2, cum, 0.0), axis=-1, keepdims=True)
    out = jnp.where(lane == 0, e1.astype(F32), 0.0)
    for i, val in enumerate((e2.astype(F32), g1, g2, r1, r2), start=1):
        out = jnp.where(lane == i, val, out)
    return out, sel


def _mixb_kernel(mix_ref, qm_ref, x_ref, wout_ref, mk_ref, mv_ref, gf_ref, rhi_ref, rlo_ref,
                 h_ref, ht_ref, route_ref, cnt_ref, carry, *, n_experts):
    @pl.when((pl.program_id(0) == 0) & (pl.program_id(1) == 0))
    def _():
        carry[...] = jnp.zeros(carry.shape, F32)

    c = mix_ref.shape[1]
    mo = _mem_attn(qm_ref[...], mk_ref[0], mv_ref[0])
    h = x_ref[...] + (_dot(mix_ref[...], wout_ref[:c, :]) + _dot(mo.astype(BF16), wout_ref[c:, :]))
    h_ref[...] = h
    for j in range(SUBLANES):
        ht_ref[_tile_rows(j, h.shape[0]), :] = h[:, j * LANES:(j + 1) * LANES]
    route, sel = _route(h, gf_ref[...], rhi_ref[...], rlo_ref[...], carry[...], n_experts)
    route_ref[...] = route
    carry[...] += jnp.sum(sel, axis=0, keepdims=True)
    cnt_ref[...] = carry[...]


def _mixb(mix_bf, qm, x2d, b, t, tm, wout_bf, mem_k, mem_v, g_ffn, router):
    m, d = x2d.shape
    assert d == SUBLANES * LANES
    c = mix_bf.shape[1]
    n_t = t // tm
    n_mem = mem_k.shape[2]
    n_experts = router.shape[1]
    r_pad = jnp.zeros((d, LANES), F32).at[:, :n_experts].set(router)
    r_hi = r_pad.astype(BF16)
    r_lo = (r_pad - r_hi.astype(F32)).astype(BF16)
    row = lambda bi, i: (bi * n_t + i, 0)
    kern = functools.partial(_mixb_kernel, n_experts=n_experts)
    return pl.pallas_call(
        kern,
        grid=(b, n_t),
        in_specs=[
            pl.BlockSpec((tm, c), row),
            pl.BlockSpec((tm, MEM_W), row),
            pl.BlockSpec((tm, d), row),
            _resident(wout_bf.shape),
            _mem_spec(n_mem),
            _mem_spec(n_mem),
            _resident((1, d)),
            _resident((d, LANES)),
            _resident((d, LANES)),
        ],
        out_specs=[
            pl.BlockSpec((tm, d), row),
            pl.BlockSpec((tm * SUBLANES, LANES), row),
            pl.BlockSpec((tm, LANES), row),
            pl.BlockSpec((1, LANES), lambda bi, i: (0, 0)),
        ],
        out_shape=[
            jax.ShapeDtypeStruct((m, d), F32),
            jax.ShapeDtypeStruct((m * SUBLANES, LANES), F32),
            jax.ShapeDtypeStruct((m, LANES), F32),
            jax.ShapeDtypeStruct((1, LANES), F32),
        ],
        scratch_shapes=[pltpu.VMEM((1, LANES), F32)],
        compiler_params=_cparams("arbitrary", "arbitrary"),
        name="mixb",
    )(mix_bf, qm, x2d, wout_bf, mem_k, mem_v, g_ffn.reshape(1, d), r_hi, r_lo)


GATHER_UNROLL = 8


def _tile_copy(src_hbm, tok, dst, slot, sem):
    def first_row(t):
        return t * SUBLANES if isinstance(t, int) else pl.multiple_of(t * SUBLANES, SUBLANES)

    return pltpu.make_async_copy(src_hbm.at[pl.ds(first_row(tok), SUBLANES)],
                                 dst.at[pl.ds(first_row(slot), SUBLANES)], sem)


def _gather_start(src_hbm, idx_ref, base, dst, sem, n):
    def issue(r, carry):
        _tile_copy(src_hbm, idx_ref[base + r], dst, r, sem).start()
        return carry

    lax.fori_loop(0, n, issue, 0, unroll=GATHER_UNROLL)


def _gather_start_inline(src_hbm, idx_ref, base, dst, sem, n):
    for r in range(n):
        _tile_copy(src_hbm, idx_ref[base + r], dst, r, sem).start()


def _gather_wait(src_hbm, dst, sem, n):
    def wait(r, carry):
        _tile_copy(src_hbm, 0, dst, r, sem).wait()
        return carry

    lax.fori_loop(0, n, wait, 0, unroll=GATHER_UNROLL)


def _expert_kernel(blk_e_ref, tok_ref, ht_hbm, g_ref, wg_ref, wu_ref, wd_ref, o_ref, xbuf, sem, *, blk, n_blk):
    i = pl.program_id(0)
    slot = i % 2

    @pl.when(i == 0)
    def _():
        _gather_start(ht_hbm, tok_ref, 0, xbuf.at[0], sem.at[0], blk)

    _gather_wait(ht_hbm, xbuf.at[slot], sem.at[slot], blk)
    chunks = [xbuf[slot, _tile_rows(j, blk), :] for j in range(SUBLANES)]
    ss = jnp.sum(functools.reduce(jnp.add, [c * c for c in chunks]), axis=-1, keepdims=True)
    inv = lax.rsqrt(ss / (SUBLANES * LANES) + EPS)
    g = g_ref[...]
    hn = jnp.concatenate([(c * inv * g[:, j * LANES:(j + 1) * LANES]).astype(BF16) for j, c in enumerate(chunks)],
                         axis=1)
    nxt = jnp.minimum(i + 1, n_blk - 1)
    _gather_start_inline(ht_hbm, tok_ref, nxt * blk, xbuf.at[1 - slot], sem.at[1 - slot], blk)
    a = _silu(_dot(hn, wg_ref[0])) * _dot(hn, wu_ref[0])
    o = _dot(a.astype(BF16), wd_ref[0])
    for j in range(SUBLANES):
        o_ref[_tile_rows(j, blk), :] = o[:, j * LANES:(j + 1) * LANES]

    @pl.when(i == n_blk - 1)
    def _():
        _gather_wait(ht_hbm, xbuf.at[1 - slot], sem.at[1 - slot], blk)


def _experts(h_tiles, g, wg_bf, wu_bf, wd_bf, blk_e, buf_tok, blk):
    n_rows = buf_tok.shape[0]
    n_blk = n_rows // blk
    _, d, ff = wg_bf.shape
    kern = functools.partial(_expert_kernel, blk=blk, n_blk=n_blk)
    grid_spec = pltpu.PrefetchScalarGridSpec(
        num_scalar_prefetch=2,
        grid=(n_blk,),
        in_specs=[
            pl.BlockSpec(memory_space=pl.ANY),
            pl.BlockSpec((1, d), lambda i, be, tk: (0, 0)),
            pl.BlockSpec((1, d, ff), lambda i, be, tk: (be[i], 0, 0)),
            pl.BlockSpec((1, d, ff), lambda i, be, tk: (be[i], 0, 0)),
            pl.BlockSpec((1, ff, d), lambda i, be, tk: (be[i], 0, 0)),
        ],
        out_specs=pl.BlockSpec((blk * SUBLANES, LANES), lambda i, be, tk: (i, 0)),
        scratch_shapes=[pltpu.VMEM((2, blk * SUBLANES, LANES), F32), pltpu.SemaphoreType.DMA((2,))],
    )
    return pl.pallas_call(
        kern,
        grid_spec=grid_spec,
        out_shape=jax.ShapeDtypeStruct((n_rows * SUBLANES, LANES), F32),
        compiler_params=_cparams("arbitrary"),
        name="experts",
    )(blk_e, buf_tok, h_tiles, g.reshape(1, d), wg_bf, wu_bf, wd_bf)


def _combine_kernel(pos_ref, h_ref, route_ref, g_ref, yb_hbm, o_ref, ybuf, sem, *, tc, n_steps):
    i = pl.program_id(0)
    slot = i % 2

    @pl.when(i == 0)
    def _():
        for k in range(TOP_K):
            _gather_start(yb_hbm, pos_ref, k * n_steps * tc, ybuf.at[0, k], sem.at[0], tc)

    for k in range(TOP_K):
        _gather_wait(yb_hbm, ybuf.at[slot, k], sem.at[slot], tc)
    nxt = jnp.minimum(i + 1, n_steps - 1)
    for k in range(TOP_K):
        _gather_start_inline(yb_hbm, pos_ref, (k * n_steps + nxt) * tc, ybuf.at[1 - slot, k], sem.at[1 - slot], tc)
    route = route_ref[...]
    g1 = route[:, 2:3]
    g2 = route[:, 3:4]
    z = [h_ref[:, j * LANES:(j + 1) * LANES]
         + (ybuf[slot, 0, _tile_rows(j, tc), :] * g1 + ybuf[slot, 1, _tile_rows(j, tc), :] * g2)
         for j in range(SUBLANES)]
    ss = jnp.sum(functools.reduce(jnp.add, [c * c for c in z]), axis=-1, keepdims=True)
    inv = lax.rsqrt(ss / (SUBLANES * LANES) + EPS)
    g = g_ref[...]
    for j in range(SUBLANES):
        o_ref[:, j * LANES:(j + 1) * LANES] = z[j] * inv * g[:, j * LANES:(j + 1) * LANES]

    @pl.when(i == n_steps - 1)
    def _():
        for k in range(TOP_K):
            _gather_wait(yb_hbm, ybuf.at[1 - slot, k], sem.at[1 - slot], tc)


def _combine(h2d, route, yb_tiles, pos_k_major, g, tc):
    m, d = h2d.shape
    n_steps = m // tc
    kern = functools.partial(_combine_kernel, tc=tc, n_steps=n_steps)
    grid_spec = pltpu.PrefetchScalarGridSpec(
        num_scalar_prefetch=1,
        grid=(n_steps,),
        in_specs=[
            pl.BlockSpec((tc, d), lambda i, p: (i, 0)),
            pl.BlockSpec((tc, LANES), lambda i, p: (i, 0)),
            pl.BlockSpec((1, d), lambda i, p: (0, 0)),
            pl.BlockSpec(memory_space=pl.ANY),
        ],
        out_specs=pl.BlockSpec((tc, d), lambda i, p: (i, 0)),
        scratch_shapes=[pltpu.VMEM((2, TOP_K, tc * SUBLANES, LANES), F32), pltpu.SemaphoreType.DMA((2,))],
    )
    return pl.pallas_call(
        kern,
        grid_spec=grid_spec,
        out_shape=jax.ShapeDtypeStruct((m, d), F32),
        compiler_params=_cparams("arbitrary"),
        name="combine",
    )(pos_k_major, h2d, route, g.reshape(1, d), yb_tiles)


def _moe(h2d, h_tiles, route, counts, tc, g_ffn, wg_bf, wu_bf, wd_bf, g_final, blk):
    m, d = h2d.shape
    n_experts = wg_bf.shape[0]
    a = m * TOP_K
    counts = counts[0, :n_experts].astype(jnp.int32)
    padded = (counts + blk - 1) // blk * blk
    cum_pad = jnp.cumsum(padded)
    top_e = route[:, :TOP_K].astype(jnp.int32)
    dest = (cum_pad - padded)[top_e] + route[:, 2 * TOP_K:3 * TOP_K].astype(jnp.int32)
    n_blk = -(-(a + n_experts * (blk - 1)) // blk)
    buf_tok = jnp.zeros((n_blk * blk,), jnp.int32).at[dest.reshape(-1)].set(jnp.arange(a, dtype=jnp.int32) // TOP_K)
    n_used = cum_pad[-1] // blk
    blk_i = jnp.minimum(jnp.arange(n_blk, dtype=jnp.int32), n_used - 1)
    blk_e = jnp.minimum(jnp.searchsorted(cum_pad, blk_i * blk, side='right'), n_experts - 1).astype(jnp.int32)
    yb = _experts(h_tiles, g_ffn, wg_bf, wu_bf, wd_bf, blk_e, buf_tok, blk)
    return _combine(h2d, route, yb, dest.T.reshape(-1), g_final, tc)


def _head_major(x, b, t):
    n_heads = x.shape[1]
    return jnp.transpose(x.reshape(n_heads, b, t, HEAD_W), (1, 0, 2, 3))


def _trunk(x, pos, conv_prev, mem_k, mem_v, attend, w, tm, tok_tm, moe_blk):
    b, t, d = x.shape
    x2d = x.reshape(b * t, d)
    h, conv_st = _mixa(x2d, b, t, tm, w['g_mix'][0], w['a_w_in'], w['a_conv'], w['a_w_out'], mem_k[0], mem_v[0],
                       conv_prev)
    h = _ffn(h, tok_tm, w['g_ffn'][0], w['f_w_gate'], w['f_w_up'], w['f_w_down'])
    cos, sin = _rope_tables(pos)
    if tok_tm > t:
        cos, sin = jnp.tile(cos, (b, 1)), jnp.tile(sin, (b, 1))
        k, v, k_bf, v_bf = (_head_major(o, b, t) for o in _kvproj(h, 1, b * t, tok_tm, w['g_kv'], w['w_kv'], cos, sin))
        q_bf, qm = _qproj(h, 1, b * t, tok_tm, w['g_mix'][1], w['b_w_in'], cos, sin)
        q_bf = _head_major(q_bf, b, t)
    else:
        k, v, k_bf, v_bf = _kvproj(h, b, t, tok_tm, w['g_kv'], w['w_kv'], cos, sin)
        q_bf, qm = _qproj(h, b, t, tok_tm, w['g_mix'][1], w['b_w_in'], cos, sin)
    mix = attend(q_bf, k, v, k_bf, v_bf)
    h, h_tiles, route, counts = _mixb(mix, qm, h, b, t, tm, w['b_w_out'], mem_k[1], mem_v[1], w['g_ffn'][1],
                                      w['m_router'])
    y = _moe(h, h_tiles, route, counts, tok_tm, w['g_ffn'][1], w['m_w_gate'], w['m_w_up'], w['m_w_down'],
             w['g_final'], moe_blk)
    return y.reshape(b, t, d), conv_st, k, v


def _token_major(x):
    return jnp.transpose(x, (0, 2, 1, 3))


def kernel(x_prompt, x_sample, state_conv, cache_k, cache_v, cache_mem_k, cache_mem_v, page_table, mem_prompt, g_mix, g_ffn, g_mem, w_mem_kv, a_w_in, a_conv, a_w_out, g_kv, w_kv, b_w_in, b_lambda, b_subln, b_w_out, f_w_gate, f_w_up, f_w_down, m_router, m_w_gate, m_w_up, m_w_down, g_final):
    bp, tp, d = x_prompt.shape
    bs, ts, _ = x_sample.shape
    depth = g_mix.shape[0]
    n_a = a_w_in.shape[0]
    assert depth == 2 and n_a == 1 and b_w_in.shape[0] == 1, "one conv layer followed by one attention layer"
    assert ts == SUBLANES and tp % 512 == 0
    c = d - MEM_W
    lam_init = 0.8 - 0.6 * math.exp(-0.3 * n_a)
    w = {
        'g_mix': g_mix, 'g_ffn': g_ffn, 'g_kv': g_kv, 'g_final': g_final,
        'a_w_in': a_w_in[0].astype(BF16), 'a_conv': a_conv[0], 'a_w_out': a_w_out[0].astype(BF16),
        'w_kv': w_kv.astype(BF16), 'b_w_in': b_w_in[0].astype(BF16), 'b_w_out': b_w_out[0].astype(BF16),
        'f_w_gate': f_w_gate[0].astype(BF16), 'f_w_up': f_w_up[0].astype(BF16), 'f_w_down': f_w_down[0].astype(BF16),
        'm_router': m_router[0], 'm_w_gate': m_w_gate[0].astype(BF16), 'm_w_up': m_w_up[0].astype(BF16),
        'm_w_down': m_w_down[0].astype(BF16),
    }
    lam_p = b_lambda[0]
    subln_g = b_subln[0]

    n_mem = mem_prompt.shape[1]
    mem_kt, mem_vt = _memkv(mem_prompt, g_mem, w_mem_kv.astype(BF16))

    def mem_out(x):
        return jnp.transpose(x.reshape(depth, bp, MEM_HEADS, MEM_HEAD_DIM, n_mem), (0, 1, 4, 2, 3))

    def attend_prompt(q_bf, k, v, k_bf, v_bf):
        return _attn_prompt(q_bf, k_bf, v_bf, 512, lam_p, subln_g, lam_init)

    y_p, conv_p, k_p, v_p = _trunk(
        x_prompt, jnp.arange(tp), jnp.zeros((bp, CONV_WIDTH - 1, c), F32), mem_kt, mem_vt, attend_prompt, w,
        tm=256, tok_tm=256, moe_blk=256)

    past = page_table.shape[1] * cache_k.shape[1]
    pages = math.gcd(page_table.shape[1], 16)
    cache_kh = jnp.transpose(cache_k, (0, 2, 1, 3))
    cache_vh = jnp.transpose(cache_v, (0, 2, 1, 3))

    def mem_in(x):
        return jnp.transpose(x, (0, 1, 3, 4, 2)).reshape(depth, bs, MEM_W, x.shape[2])

    def attend_sample(q_bf, k, v, k_bf, v_bf):
        return _attn_sample(q_bf, k, v, cache_kh, cache_vh, page_table, pages, lam_p, subln_g, lam_init)

    y_s, conv_s, k_s, v_s = _trunk(
        x_sample, past + jnp.arange(ts), state_conv[0], mem_in(cache_mem_k), mem_in(cache_mem_v), attend_sample, w,
        tm=ts, tok_tm=bs * ts, moe_blk=128)

    return (y_p, y_s, conv_p[None], conv_s[None], _token_major(k_p), _token_major(v_p), _token_major(k_s),
            _token_major(v_s), mem_out(mem_kt), mem_out(mem_vt))
```

```python
import functools
import math

import jax
import jax.numpy as jnp
from jax import lax
from jax.experimental import pallas as pl
from jax.experimental.pallas import tpu as pltpu

F32 = jnp.float32
BF16 = jnp.bfloat16

EPS = 1e-6
MEM_HEADS = 4
MEM_HEAD_DIM = 64
MEM_W = MEM_HEADS * MEM_HEAD_DIM
HEAD_W = 128
HALF_W = HEAD_W // 2
ROPE_HALF = HALF_W // 2
ROPE_THETA = 10000.0
CONV_WIDTH = 3
TOP_K = 2
NEG = -1e30
QK_SCALE = HALF_W ** -0.5 * math.log2(math.e)
MEM_SCALE = MEM_HEAD_DIM ** -0.5
LANES = 128
SUBLANES = 8
VMEM_LIMIT = 56 * 1024 * 1024


def _cparams(*sem):
    return pltpu.CompilerParams(dimension_semantics=sem, vmem_limit_bytes=VMEM_LIMIT)


def _rms(x, g):
    return x * lax.rsqrt(jnp.mean(x * x, axis=-1, keepdims=True) + EPS) * g


def _dot(a, b):
    return jnp.dot(a, b, preferred_element_type=F32)


def _dot_t(a, b):
    return lax.dot_general(a, b, (((1,), (1,)), ((), ())), preferred_element_type=F32)


def _resident(shape):
    n = len(shape)
    return pl.BlockSpec(shape, lambda *_: (0,) * n, pipeline_mode=pl.Buffered(1))


def _mem_attn(qm, mk_t, mv_t):
    q = (qm * MEM_SCALE).astype(BF16)
    kb = mk_t.astype(BF16)
    vb = mv_t.astype(BF16)
    lane = lax.broadcasted_iota(jnp.int32, q.shape, 1)
    out = jnp.zeros(q.shape, F32)
    for h in range(MEM_HEADS):
        in_head = (lane >= h * MEM_HEAD_DIM) & (lane < (h + 1) * MEM_HEAD_DIM)
        s = _dot(jnp.where(in_head, q, jnp.zeros_like(q)), kb)
        m = jnp.max(s, axis=-1, keepdims=True)
        p = jnp.exp(s - m)
        p = p / jnp.sum(p, axis=-1, keepdims=True)
        out = jnp.where(in_head, _dot_t(p.astype(BF16), vb), out)
    return out


def _rope_slab(x, cos, sin_signed):
    lane = lax.broadcasted_iota(jnp.int32, x.shape, 1)
    first_half = (lane % HALF_W) < ROPE_HALF
    partner = jnp.where(first_half, pltpu.roll(x, HEAD_W - ROPE_HALF, 1), pltpu.roll(x, ROPE_HALF, 1))
    return x * cos + partner * sin_signed


def _lam(lam_ref, lam_init):
    lp = lam_ref[...]
    a = jnp.sum(lp[0:1] * lp[1:2], axis=-1, keepdims=True)
    b = jnp.sum(lp[2:3] * lp[3:4], axis=-1, keepdims=True)
    return jnp.exp(a) - jnp.exp(b) + lam_init


def _subln(o, g, lam_init):
    o = o * lax.rsqrt(jnp.mean(o * o, axis=-1, keepdims=True) + EPS)
    return o * g * (1.0 - lam_init)


def _memkv_kernel(x_ref, g_ref, w_ref, k_ref, v_ref):
    hn = _rms(x_ref[0], g_ref[0]).astype(BF16)
    kv_t = _dot(hn, w_ref[0]).T
    k_ref[0, 0] = kv_t[:MEM_W]
    v_ref[0, 0] = kv_t[MEM_W:]


def _memkv(mem, g_mem, w_bf):
    depth, d, n = w_bf.shape
    b, n_mem, _ = mem.shape
    out = jax.ShapeDtypeStruct((depth, b, MEM_W, n_mem), F32)
    return pl.pallas_call(
        _memkv_kernel,
        grid=(depth, b),
        in_specs=[
            pl.BlockSpec((1, n_mem, d), lambda l, bi: (bi, 0, 0)),
            pl.BlockSpec((1, 1, d), lambda l, bi: (l, 0, 0)),
            pl.BlockSpec((1, d, n), lambda l, bi: (l, 0, 0)),
        ],
        out_specs=[pl.BlockSpec((1, 1, MEM_W, n_mem), lambda l, bi: (l, bi, 0, 0))] * 2,
        out_shape=[out, out],
        compiler_params=_cparams("arbitrary", "arbitrary"),
        name="memkv",
    )(mem, g_mem.reshape(depth, 1, d), w_bf)


def _mixa_kernel(x_ref, g_ref, win_ref, cw_ref, wout_ref, mk_ref, mv_ref, prev_ref, h_ref, st_ref, ubuf,
                 *, tm, c, n_t):
    i = pl.program_id(1)
    x = x_ref[...]
    hn = _rms(x, g_ref[...]).astype(BF16)
    proj = _dot(hn, win_ref[...])
    gate_b = proj[:, :c]
    u = proj[:, c:2 * c] * proj[:, 2 * c:3 * c]
    qm = proj[:, 3 * c:]

    @pl.when(i == 0)
    def _():
        ubuf[SUBLANES - 2:SUBLANES, :] = prev_ref[0]

    @pl.when(i > 0)
    def _():
        ubuf[SUBLANES - 2:SUBLANES, :] = ubuf[tm + SUBLANES - 2:tm + SUBLANES, :]

    ubuf[SUBLANES:SUBLANES + tm, :] = u
    cw = cw_ref[...]
    y = cw[0:1] * ubuf[SUBLANES - 2:SUBLANES - 2 + tm, :]
    y = y + cw[1:2] * ubuf[SUBLANES - 1:SUBLANES - 1 + tm, :]
    y = y + cw[2:3] * u
    mix = gate_b * y
    mo = _mem_attn(qm, mk_ref[0], mv_ref[0])
    out = _dot(mix.astype(BF16), wout_ref[:c, :]) + _dot(mo.astype(BF16), wout_ref[c:, :])
    h_ref[...] = x + out

    @pl.when(i == n_t - 1)
    def _():
        st_ref[0] = ubuf[tm + SUBLANES - 2:tm + SUBLANES, :]


def _mem_spec(n_mem):
    return pl.BlockSpec((1, MEM_W, n_mem), lambda bi, i: (bi, 0, 0))


def _mixa(x2d, b, t, tm, g, win_bf, conv_w, wout_bf, mem_k, mem_v, prev):
    d = x2d.shape[1]
    c = d - MEM_W
    n_t = t // tm
    n_mem = mem_k.shape[2]
    kern = functools.partial(_mixa_kernel, tm=tm, c=c, n_t=n_t)
    return pl.pallas_call(
        kern,
        grid=(b, n_t),
        in_specs=[
            pl.BlockSpec((tm, d), lambda bi, i: (bi * n_t + i, 0)),
            _resident((1, d)),
            _resident(win_bf.shape),
            _resident(conv_w.shape),
            _resident(wout_bf.shape),
            _mem_spec(n_mem),
            _mem_spec(n_mem),
            pl.BlockSpec((1, CONV_WIDTH - 1, c), lambda bi, i: (bi, 0, 0)),
        ],
        out_specs=[
            pl.BlockSpec((tm, d), lambda bi, i: (bi * n_t + i, 0)),
            pl.BlockSpec((1, CONV_WIDTH - 1, c), lambda bi, i: (bi, 0, 0)),
        ],
        out_shape=[
            jax.ShapeDtypeStruct(x2d.shape, F32),
            jax.ShapeDtypeStruct((b, CONV_WIDTH - 1, c), F32),
        ],
        scratch_shapes=[pltpu.VMEM((tm + SUBLANES, c), F32)],
        compiler_params=_cparams("arbitrary", "arbitrary"),
        name="mixa",
    )(x2d, g.reshape(1, d), win_bf, conv_w, wout_bf, mem_k, mem_v, prev)


def _silu(g):
    return g / (1.0 + jnp.exp(-g))


def _ffn_kernel(h_ref, g_ref, wg_ref, wu_ref, wd_ref, o_ref):
    h = h_ref[...]
    hn = _rms(h, g_ref[...]).astype(BF16)
    a = _silu(_dot(hn, wg_ref[...])) * _dot(hn, wu_ref[...])
    o_ref[...] = h + _dot(a.astype(BF16), wd_ref[...])


def _ffn(h2d, tm, g, wg_bf, wu_bf, wd_bf):
    m, d = h2d.shape
    return pl.pallas_call(
        _ffn_kernel,
        grid=(m // tm,),
        in_specs=[
            pl.BlockSpec((tm, d), lambda i: (i, 0)),
            _resident((1, d)),
            _resident(wg_bf.shape),
            _resident(wu_bf.shape),
            _resident(wd_bf.shape),
        ],
        out_specs=pl.BlockSpec((tm, d), lambda i: (i, 0)),
        out_shape=jax.ShapeDtypeStruct(h2d.shape, F32),
        compiler_params=_cparams("arbitrary"),
        name="ffn",
    )(h2d, g.reshape(1, d), wg_bf, wu_bf, wd_bf)


def _kv_kernel(h_ref, g_ref, w_ref, cos_ref, sin_ref, k_ref, v_ref, kb_ref, vb_ref, *, n_heads):
    hn = _rms(h_ref[...], g_ref[...]).astype(BF16)
    kv = _dot(hn, w_ref[...])
    cos = cos_ref[...]
    sin = sin_ref[...]
    for h in range(n_heads):
        r = _rope_slab(kv[:, h * HEAD_W:(h + 1) * HEAD_W], cos, sin)
        k_ref[0, h] = r
        kb_ref[0, h] = r.astype(BF16)
        v = kv[:, (n_heads + h) * HEAD_W:(n_heads + h + 1) * HEAD_W]
        v_ref[0, h] = v
        vb_ref[0, h] = v.astype(BF16)


def _q_kernel(h_ref, g_ref, w_ref, cos_ref, sin_ref, q_ref, qm_ref, *, n_heads):
    hn = _rms(h_ref[...], g_ref[...]).astype(BF16)
    p = _dot(hn, w_ref[...])
    cos = cos_ref[...]
    sin = sin_ref[...]
    for h in range(n_heads):
        q_ref[0, h] = (_rope_slab(p[:, h * HEAD_W:(h + 1) * HEAD_W], cos, sin) * QK_SCALE).astype(BF16)
    qm_ref[...] = p[:, n_heads * HEAD_W:]


def _rope_tables(pos):
    inv = ROPE_THETA ** (-jnp.arange(ROPE_HALF, dtype=F32) / ROPE_HALF)
    ang = pos.astype(F32)[:, None] * inv[None, :]
    cos = jnp.tile(jnp.cos(ang), (1, HEAD_W // ROPE_HALF))
    sin = jnp.sin(ang)
    sin = jnp.tile(jnp.concatenate([-sin, sin], axis=-1), (1, HEAD_W // HALF_W))
    return cos, sin


def _kvproj(h2d, b, t, tm, g, w_bf, cos, sin):
    m, d = h2d.shape
    n_heads = w_bf.shape[1] // (2 * HEAD_W)
    n_t = t // tm
    kern = functools.partial(_kv_kernel, n_heads=n_heads)
    head_major = pl.BlockSpec((1, n_heads, tm, HEAD_W), lambda bi, i: (bi, 0, i, 0))
    rope = pl.BlockSpec((tm, HEAD_W), lambda bi, i: (i, 0))
    return pl.pallas_call(
        kern,
        grid=(b, n_t),
        in_specs=[pl.BlockSpec((tm, d), lambda bi, i: (bi * n_t + i, 0)), _resident((1, d)), _resident(w_bf.shape),
                  rope, rope],
        out_specs=[head_major] * 4,
        out_shape=[jax.ShapeDtypeStruct((b, n_heads, t, HEAD_W), F32)] * 2
        + [jax.ShapeDtypeStruct((b, n_heads, t, HEAD_W), BF16)] * 2,
        compiler_params=_cparams("arbitrary", "arbitrary"),
        name="kvproj",
    )(h2d, g.reshape(1, d), w_bf, cos, sin)


def _qproj(h2d, b, t, tm, g, w_bf, cos, sin):
    m, d = h2d.shape
    n_heads = (w_bf.shape[1] - MEM_W) // HEAD_W
    n_t = t // tm
    kern = functools.partial(_q_kernel, n_heads=n_heads)
    rope = pl.BlockSpec((tm, HEAD_W), lambda bi, i: (i, 0))
    return pl.pallas_call(
        kern,
        grid=(b, n_t),
        in_specs=[pl.BlockSpec((tm, d), lambda bi, i: (bi * n_t + i, 0)), _resident((1, d)), _resident(w_bf.shape),
                  rope, rope],
        out_specs=[pl.BlockSpec((1, n_heads, tm, HEAD_W), lambda bi, i: (bi, 0, i, 0)),
                   pl.BlockSpec((tm, MEM_W), lambda bi, i: (bi * n_t + i, 0))],
        out_shape=[jax.ShapeDtypeStruct((b, n_heads, t, HEAD_W), BF16), jax.ShapeDtypeStruct((m, MEM_W), F32)],
        compiler_params=_cparams("arbitrary", "arbitrary"),
        name="qproj",
    )(h2d, g.reshape(1, d), w_bf, cos, sin)


def _stack_components(q):
    lane = lax.broadcasted_iota(jnp.int32, q.shape, 1)
    zero = jnp.zeros_like(q)
    return jnp.concatenate([jnp.where(lane < HALF_W, q, zero), jnp.where(lane >= HALF_W, q, zero)], axis=0)


def _online_update(s, pv, m_ref, l_ref, acc_ref):
    cols = [s[:, c * LANES:(c + 1) * LANES] for c in range(s.shape[1] // LANES)]
    m_prev = m_ref[...]
    m_new = jnp.maximum(m_prev, jnp.max(functools.reduce(jnp.maximum, cols), axis=-1, keepdims=True))
    alpha = jnp.exp2(m_prev - m_new)
    ps = [jnp.exp2(c - m_new) for c in cols]
    l_ref[...] = alpha * l_ref[...] + functools.reduce(jnp.add, ps)
    acc_ref[...] = alpha * acc_ref[...] + pv(jnp.concatenate(ps, axis=1).astype(BF16))
    m_ref[...] = m_new


def _init_online(m_ref, l_ref, acc_ref):
    m_ref[...] = jnp.full(m_ref.shape, NEG, F32)
    l_ref[...] = jnp.zeros(l_ref.shape, F32)
    acc_ref[...] = jnp.zeros(acc_ref.shape, F32)


def _diff_combine(l_ref, acc_ref, t, lam):
    o = acc_ref[...] / jnp.sum(l_ref[...], axis=-1, keepdims=True)
    return o[:t] - lam * o[t:]


def _attn_prompt_kernel(q_ref, k_ref, v_ref, lam_ref, sg_ref, o_ref, m_ref, l_ref, acc_ref, *, tq, lam_init):
    qi = pl.program_id(2)
    qs = _stack_components(q_ref[0, 0])
    _init_online(m_ref, l_ref, acc_ref)

    def chunk(j, masked):
        start = pl.multiple_of(j * tq, tq)
        s = _dot_t(qs, k_ref[0, 0, pl.ds(start, tq), :])
        if masked:
            row = lax.broadcasted_iota(jnp.int32, s.shape, 0) % tq
            col = lax.broadcasted_iota(jnp.int32, s.shape, 1)
            s = jnp.where(col <= row, s, NEG)
        v = v_ref[0, 0, pl.ds(start, tq), :]
        _online_update(s, lambda p: _dot(p, v), m_ref, l_ref, acc_ref)

    def body(j, carry):
        chunk(2 * j, False)
        chunk(2 * j + 1, False)
        return carry

    lax.fori_loop(0, qi // 2, body, 0)

    @pl.when(qi % 2 == 1)
    def _():
        chunk(qi - 1, False)

    chunk(qi, True)
    o = _diff_combine(l_ref, acc_ref, tq, _lam(lam_ref, lam_init))
    o_ref[...] = _subln(o, sg_ref[...], lam_init).astype(BF16)


def _attn_prompt(q_bf, k_bf, v_bf, tq, lam_p, subln_g, lam_init):
    b, n_heads, t, _ = q_bf.shape
    nq = t // tq
    kern = functools.partial(_attn_prompt_kernel, tq=tq, lam_init=lam_init)
    seq = pl.BlockSpec((1, 1, t, HEAD_W), lambda bi, h, qi: (bi, h, 0, 0))
    return pl.pallas_call(
        kern,
        grid=(b, n_heads, nq),
        in_specs=[
            pl.BlockSpec((1, 1, tq, HEAD_W), lambda bi, h, qi: (bi, h, qi, 0)),
            seq,
            seq,
            pl.BlockSpec(lam_p.shape, lambda bi, h, qi: (0, 0)),
            pl.BlockSpec((1, HEAD_W), lambda bi, h, qi: (0, 0)),
        ],
        out_specs=pl.BlockSpec((tq, HEAD_W), lambda bi, h, qi: (bi * nq + qi, h)),
        out_shape=jax.ShapeDtypeStruct((b * t, n_heads * HEAD_W), BF16),
        scratch_shapes=[pltpu.VMEM((2 * tq, LANES), F32)] * 3,
        compiler_params=_cparams("arbitrary", "arbitrary", "arbitrary"),
        name="attn_prompt",
    )(q_bf, k_bf, v_bf, lam_p, subln_g.reshape(1, HEAD_W))


def _attn_sample_kernel(pt_ref, q_ref, kn_ref, vn_ref, *rest, ts, n_heads, pages, n_groups, lam_init):
    k_refs = rest[:pages]
    v_refs = rest[pages:2 * pages]
    lam_ref, sg_ref, o_ref, qs_ref, m_ref, l_ref, acc_ref = rest[2 * pages:]
    g = pl.program_id(1)

    r = 2 * ts

    @pl.when(g == 0)
    def _():
        for h in range(n_heads):
            qs_ref[h] = _stack_components(q_ref[0, h])
        _init_online(m_ref, l_ref, acc_ref)

    def all_heads(keys, values):
        s = jnp.concatenate([_dot_t(qs_ref[h], keys(h)) for h in range(n_heads)], axis=0)

        def pv(p):
            return jnp.concatenate([_dot(p[h * r:(h + 1) * r], values(h)) for h in range(n_heads)], axis=0)

        return s, pv

    def past(refs):
        return lambda h: jnp.concatenate([x[0, h].astype(BF16) for x in refs], axis=0)

    s, pv = all_heads(past(k_refs), past(v_refs))
    _online_update(s, pv, m_ref, l_ref, acc_ref)

    @pl.when(g == n_groups - 1)
    def _():
        pad = jnp.zeros((LANES - ts, HEAD_W), BF16)

        def new(ref):
            return lambda h: jnp.concatenate([ref[0, h].astype(BF16), pad], axis=0)

        s, pv = all_heads(new(kn_ref), new(vn_ref))
        row = lax.broadcasted_iota(jnp.int32, s.shape, 0) % ts
        col = lax.broadcasted_iota(jnp.int32, s.shape, 1)
        _online_update(jnp.where(col <= row, s, NEG), pv, m_ref, l_ref, acc_ref)
        lam = _lam(lam_ref, lam_init)
        for h in range(n_heads):
            rows = pl.ds(h * r, r)
            o = _diff_combine(l_ref.at[rows], acc_ref.at[rows], ts, lam)
            o_ref[:, h * HEAD_W:(h + 1) * HEAD_W] = _subln(o, sg_ref[...], lam_init).astype(BF16)


def _attn_sample(q_bf, k_new, v_new, cache_k, cache_v, page_table, pages, lam_p, subln_g, lam_init):
    bs, n_heads, ts, _ = q_bf.shape
    ps = cache_k.shape[2]
    n_groups = page_table.shape[1] // pages
    kern = functools.partial(_attn_sample_kernel, ts=ts, n_heads=n_heads, pages=pages, n_groups=n_groups,
                             lam_init=lam_init)

    def page_spec(i):
        return pl.BlockSpec((1, n_heads, ps, HEAD_W), lambda bi, g, pt: (pt[bi, g * pages + i], 0, 0, 0))

    new_rows = pl.BlockSpec((1, n_heads, ts, HEAD_W), lambda bi, g, pt: (bi, 0, 0, 0))
    grid_spec = pltpu.PrefetchScalarGridSpec(
        num_scalar_prefetch=1,
        grid=(bs, n_groups),
        in_specs=[new_rows] * 3
        + [page_spec(i) for i in range(pages)] * 2
        + [pl.BlockSpec(lam_p.shape, lambda bi, g, pt: (0, 0)), pl.BlockSpec((1, HEAD_W), lambda bi, g, pt: (0, 0))],
        out_specs=pl.BlockSpec((ts, n_heads * HEAD_W), lambda bi, g, pt: (bi, 0)),
        scratch_shapes=[pltpu.VMEM((n_heads, 2 * ts, HEAD_W), BF16)]
        + [pltpu.VMEM((n_heads * 2 * ts, LANES), F32)] * 3,
    )
    return pl.pallas_call(
        kern,
        grid_spec=grid_spec,
        out_shape=jax.ShapeDtypeStruct((bs * ts, n_heads * HEAD_W), BF16),
        compiler_params=_cparams("arbitrary", "arbitrary"),
        name="attn_sample",
    )(page_table, q_bf, k_new, v_new, *([cache_k] * pages), *([cache_v] * pages), lam_p, subln_g.reshape(1, HEAD_W))


def _route(h, g, r_hi, r_lo, base, n_experts):
    hn = _rms(h, g)
    hi = hn.astype(BF16)
    lo = (hn - hi.astype(F32)).astype(BF16)
    logits = _dot(hi, r_hi) + (_dot(hi, r_lo) + _dot(lo, r_hi))
    lane = lax.broadcasted_iota(jnp.int32, logits.shape, 1)
    logits = jnp.where(lane < n_experts, logits, -jnp.inf)
    v1 = jnp.max(logits, axis=-1, keepdims=True)
    e1 = jnp.min(jnp.where(logits == v1, lane, LANES), axis=-1, keepdims=True)
    rest = jnp.where(lane == e1, -jnp.inf, logits)
    v2 = jnp.max(rest, axis=-1, keepdims=True)
    e2 = jnp.min(jnp.where(rest == v2, lane, LANES), axis=-1, keepdims=True)
    ex = jnp.exp(v2 - v1)
    g1 = 1.0 / (1.0 + ex)
    g2 = ex / (1.0 + ex)
    sel = jnp.where((lane == e1) | (lane == e2), 1.0, 0.0)
    t = h.shape[0]
    earlier = lax.broadcasted_iota(jnp.int32, (t, t), 0) > lax.broadcasted_iota(jnp.int32, (t, t), 1)
    cum = _dot(jnp.where(earlier, 1.0, 0.0).astype(BF16), sel.astype(BF16)) + base
    r1 = jnp.sum(jnp.where(lane == e1, cum, 0.0), axis=-1, keepdims=True)
    r2 = jnp.sum(jnp.where(lane == e2, cum, 0.0), axis=-1, keepdims=True)
    out = jnp.where(lane == 0, e1.astype(F32), 0.0)
    for i, val in enumerate((e2.astype(F32), g1, g2, r1, r2), start=1):
        out = jnp.where(lane == i, val, out)
    return out, sel


def _mixb_kernel(mix_ref, qm_ref, x_ref, wout_ref, mk_ref, mv_ref, gf_ref, rhi_ref, rlo_ref,
                 h_ref, route_ref, cnt_ref, carry, *, n_experts):
    @pl.when((pl.program_id(0) == 0) & (pl.program_id(1) == 0))
    def _():
        carry[...] = jnp.zeros(carry.shape, F32)

    c = mix_ref.shape[1]
    mo = _mem_attn(qm_ref[...], mk_ref[0], mv_ref[0])
    h = x_ref[...] + (_dot(mix_ref[...], wout_ref[:c, :]) + _dot(mo.astype(BF16), wout_ref[c:, :]))
    h_ref[...] = h
    route, sel = _route(h, gf_ref[...], rhi_ref[...], rlo_ref[...], carry[...], n_experts)
    route_ref[...] = route
    carry[...] += jnp.sum(sel, axis=0, keepdims=True)
    cnt_ref[...] = carry[...]


def _mixb(mix_bf, qm, x2d, b, t, tm, wout_bf, mem_k, mem_v, g_ffn, router):
    m, d = x2d.shape
    c = mix_bf.shape[1]
    n_t = t // tm
    n_mem = mem_k.shape[2]
    n_experts = router.shape[1]
    r_pad = jnp.zeros((d, LANES), F32).at[:, :n_experts].set(router)
    r_hi = r_pad.astype(BF16)
    r_lo = (r_pad - r_hi.astype(F32)).astype(BF16)
    row = lambda bi, i: (bi * n_t + i, 0)
    kern = functools.partial(_mixb_kernel, n_experts=n_experts)
    return pl.pallas_call(
        kern,
        grid=(b, n_t),
        in_specs=[
            pl.BlockSpec((tm, c), row),
            pl.BlockSpec((tm, MEM_W), row),
            pl.BlockSpec((tm, d), row),
            _resident(wout_bf.shape),
            _mem_spec(n_mem),
            _mem_spec(n_mem),
            _resident((1, d)),
            _resident((d, LANES)),
            _resident((d, LANES)),
        ],
        out_specs=[
            pl.BlockSpec((tm, d), row),
            pl.BlockSpec((tm, LANES), row),
            pl.BlockSpec((1, LANES), lambda bi, i: (0, 0)),
        ],
        out_shape=[
            jax.ShapeDtypeStruct((m, d), F32),
            jax.ShapeDtypeStruct((m, LANES), F32),
            jax.ShapeDtypeStruct((1, LANES), F32),
        ],
        scratch_shapes=[pltpu.VMEM((1, LANES), F32)],
        compiler_params=_cparams("arbitrary", "arbitrary"),
        name="mixb",
    )(mix_bf, qm, x2d, wout_bf, mem_k, mem_v, g_ffn.reshape(1, d), r_hi, r_lo)


GATHER_UNROLL = 8


def _row_copy(src_hbm, row, dst, slot, sem):
    return pltpu.make_async_copy(src_hbm.at[pl.ds(row, 1)], dst.at[pl.ds(slot, 1)], sem)


def _gather_start(src_hbm, idx_ref, base, dst, sem, n):
    def issue(r, carry):
        _row_copy(src_hbm, idx_ref[base + r], dst, r, sem).start()
        return carry

    lax.fori_loop(0, n, issue, 0, unroll=GATHER_UNROLL)


def _gather_start_inline(src_hbm, idx_ref, base, dst, sem, n):
    for r in range(n):
        _row_copy(src_hbm, idx_ref[base + r], dst, r, sem).start()


def _gather_wait(src_hbm, dst, sem, n):
    def wait(r, carry):
        _row_copy(src_hbm, 0, dst, r, sem).wait()
        return carry

    lax.fori_loop(0, n, wait, 0, unroll=GATHER_UNROLL)


def _expert_kernel(blk_e_ref, tok_ref, h_hbm, g_ref, wg_ref, wu_ref, wd_ref, o_ref, xbuf, sem, *, blk, n_blk):
    i = pl.program_id(0)
    slot = i % 2

    @pl.when(i == 0)
    def _():
        _gather_start(h_hbm, tok_ref, 0, xbuf.at[0], sem.at[0], blk)

    _gather_wait(h_hbm, xbuf.at[slot], sem.at[slot], blk)
    hn = _rms(xbuf[slot], g_ref[...]).astype(BF16)
    nxt = jnp.minimum(i + 1, n_blk - 1)
    _gather_start_inline(h_hbm, tok_ref, nxt * blk, xbuf.at[1 - slot], sem.at[1 - slot], blk)
    a = _silu(_dot(hn, wg_ref[0])) * _dot(hn, wu_ref[0])
    o_ref[...] = _dot(a.astype(BF16), wd_ref[0])

    @pl.when(i == n_blk - 1)
    def _():
        _gather_wait(h_hbm, xbuf.at[1 - slot], sem.at[1 - slot], blk)


def _experts(h2d, g, wg_bf, wu_bf, wd_bf, blk_e, buf_tok, blk):
    n_rows = buf_tok.shape[0]
    n_blk = n_rows // blk
    _, d, ff = wg_bf.shape
    kern = functools.partial(_expert_kernel, blk=blk, n_blk=n_blk)
    grid_spec = pltpu.PrefetchScalarGridSpec(
        num_scalar_prefetch=2,
        grid=(n_blk,),
        in_specs=[
            pl.BlockSpec(memory_space=pl.ANY),
            pl.BlockSpec((1, d), lambda i, be, tk: (0, 0)),
            pl.BlockSpec((1, d, ff), lambda i, be, tk: (be[i], 0, 0)),
            pl.BlockSpec((1, d, ff), lambda i, be, tk: (be[i], 0, 0)),
            pl.BlockSpec((1, ff, d), lambda i, be, tk: (be[i], 0, 0)),
        ],
        out_specs=pl.BlockSpec((blk, d), lambda i, be, tk: (i, 0)),
        scratch_shapes=[pltpu.VMEM((2, blk, d), F32), pltpu.SemaphoreType.DMA((2,))],
    )
    return pl.pallas_call(
        kern,
        grid_spec=grid_spec,
        out_shape=jax.ShapeDtypeStruct((n_rows, d), F32),
        compiler_params=_cparams("arbitrary"),
        name="experts",
    )(blk_e, buf_tok, h2d, g.reshape(1, d), wg_bf, wu_bf, wd_bf)


def _combine_kernel(pos_ref, h_ref, route_ref, g_ref, yb_hbm, o_ref, ybuf, sem, *, tc, n_steps):
    i = pl.program_id(0)
    slot = i % 2

    @pl.when(i == 0)
    def _():
        for k in range(TOP_K):
            _gather_start(yb_hbm, pos_ref, k * n_steps * tc, ybuf.at[0, k], sem.at[0], tc)

    for k in range(TOP_K):
        _gather_wait(yb_hbm, ybuf.at[slot, k], sem.at[slot], tc)
    nxt = jnp.minimum(i + 1, n_steps - 1)
    for k in range(TOP_K):
        _gather_start_inline(yb_hbm, pos_ref, (k * n_steps + nxt) * tc, ybuf.at[1 - slot, k], sem.at[1 - slot], tc)
    route = route_ref[...]
    y = ybuf[slot, 0] * route[:, 2:3] + ybuf[slot, 1] * route[:, 3:4]
    o_ref[...] = _rms(h_ref[...] + y, g_ref[...])

    @pl.when(i == n_steps - 1)
    def _():
        for k in range(TOP_K):
            _gather_wait(yb_hbm, ybuf.at[1 - slot, k], sem.at[1 - slot], tc)


def _combine(h2d, route, yb, pos_k_major, g, tc):
    m, d = h2d.shape
    n_steps = m // tc
    kern = functools.partial(_combine_kernel, tc=tc, n_steps=n_steps)
    grid_spec = pltpu.PrefetchScalarGridSpec(
        num_scalar_prefetch=1,
        grid=(n_steps,),
        in_specs=[
            pl.BlockSpec((tc, d), lambda i, p: (i, 0)),
            pl.BlockSpec((tc, LANES), lambda i, p: (i, 0)),
            pl.BlockSpec((1, d), lambda i, p: (0, 0)),
            pl.BlockSpec(memory_space=pl.ANY),
        ],
        out_specs=pl.BlockSpec((tc, d), lambda i, p: (i, 0)),
        scratch_shapes=[pltpu.VMEM((2, TOP_K, tc, d), F32), pltpu.SemaphoreType.DMA((2,))],
    )
    return pl.pallas_call(
        kern,
        grid_spec=grid_spec,
        out_shape=jax.ShapeDtypeStruct((m, d), F32),
        compiler_params=_cparams("arbitrary"),
        name="combine",
    )(pos_k_major, h2d, route, g.reshape(1, d), yb)


def _moe(h2d, route, counts, tc, g_ffn, wg_bf, wu_bf, wd_bf, g_final, blk):
    m, d = h2d.shape
    n_experts = wg_bf.shape[0]
    a = m * TOP_K
    counts = counts[0, :n_experts].astype(jnp.int32)
    padded = (counts + blk - 1) // blk * blk
    cum_pad = jnp.cumsum(padded)
    top_e = route[:, :TOP_K].astype(jnp.int32)
    dest = (cum_pad - padded)[top_e] + route[:, 2 * TOP_K:3 * TOP_K].astype(jnp.int32)
    n_blk = -(-(a + n_experts * (blk - 1)) // blk)
    buf_tok = jnp.zeros((n_blk * blk,), jnp.int32).at[dest.reshape(-1)].set(jnp.arange(a, dtype=jnp.int32) // TOP_K)
    n_used = cum_pad[-1] // blk
    blk_i = jnp.minimum(jnp.arange(n_blk, dtype=jnp.int32), n_used - 1)
    blk_e = jnp.minimum(jnp.searchsorted(cum_pad, blk_i * blk, side='right'), n_experts - 1).astype(jnp.int32)
    yb = _experts(h2d, g_ffn, wg_bf, wu_bf, wd_bf, blk_e, buf_tok, blk)
    return _combine(h2d, route, yb, dest.T.reshape(-1), g_final, tc)


def _head_major(x, b, t):
    n_heads = x.shape[1]
    return jnp.transpose(x.reshape(n_heads, b, t, HEAD_W), (1, 0, 2, 3))


def _trunk(x, pos, conv_prev, mem_k, mem_v, attend, w, tm, tok_tm, moe_blk):
    b, t, d = x.shape
    x2d = x.reshape(b * t, d)
    h, conv_st = _mixa(x2d, b, t, tm, w['g_mix'][0], w['a_w_in'], w['a_conv'], w['a_w_out'], mem_k[0], mem_v[0],
                       conv_prev)
    h = _ffn(h, tok_tm, w['g_ffn'][0], w['f_w_gate'], w['f_w_up'], w['f_w_down'])
    cos, sin = _rope_tables(pos)
    if tok_tm > t:
        cos, sin = jnp.tile(cos, (b, 1)), jnp.tile(sin, (b, 1))
        k, v, k_bf, v_bf = (_head_major(o, b, t) for o in _kvproj(h, 1, b * t, tok_tm, w['g_kv'], w['w_kv'], cos, sin))
        q_bf, qm = _qproj(h, 1, b * t, tok_tm, w['g_mix'][1], w['b_w_in'], cos, sin)
        q_bf = _head_major(q_bf, b, t)
    else:
        k, v, k_bf, v_bf = _kvproj(h, b, t, tok_tm, w['g_kv'], w['w_kv'], cos, sin)
        q_bf, qm = _qproj(h, b, t, tok_tm, w['g_mix'][1], w['b_w_in'], cos, sin)
    mix = attend(q_bf, k, v, k_bf, v_bf)
    h, route, counts = _mixb(mix, qm, h, b, t, tm, w['b_w_out'], mem_k[1], mem_v[1], w['g_ffn'][1], w['m_router'])
    y = _moe(h, route, counts, tok_tm, w['g_ffn'][1], w['m_w_gate'], w['m_w_up'], w['m_w_down'], w['g_final'],
             moe_blk)
    return y.reshape(b, t, d), conv_st, k, v


def _token_major(x):
    return jnp.transpose(x, (0, 2, 1, 3))


def kernel(x_prompt, x_sample, state_conv, cache_k, cache_v, cache_mem_k, cache_mem_v, page_table, mem_prompt, g_mix, g_ffn, g_mem, w_mem_kv, a_w_in, a_conv, a_w_out, g_kv, w_kv, b_w_in, b_lambda, b_subln, b_w_out, f_w_gate, f_w_up, f_w_down, m_router, m_w_gate, m_w_up, m_w_down, g_final):
    bp, tp, d = x_prompt.shape
    bs, ts, _ = x_sample.shape
    depth = g_mix.shape[0]
    n_a = a_w_in.shape[0]
    assert depth == 2 and n_a == 1 and b_w_in.shape[0] == 1, "one conv layer followed by one attention layer"
    assert ts == SUBLANES and tp % 512 == 0
    c = d - MEM_W
    lam_init = 0.8 - 0.6 * math.exp(-0.3 * n_a)
    w = {
        'g_mix': g_mix, 'g_ffn': g_ffn, 'g_kv': g_kv, 'g_final': g_final,
        'a_w_in': a_w_in[0].astype(BF16), 'a_conv': a_conv[0], 'a_w_out': a_w_out[0].astype(BF16),
        'w_kv': w_kv.astype(BF16), 'b_w_in': b_w_in[0].astype(BF16), 'b_w_out': b_w_out[0].astype(BF16),
        'f_w_gate': f_w_gate[0].astype(BF16), 'f_w_up': f_w_up[0].astype(BF16), 'f_w_down': f_w_down[0].astype(BF16),
        'm_router': m_router[0], 'm_w_gate': m_w_gate[0].astype(BF16), 'm_w_up': m_w_up[0].astype(BF16),
        'm_w_down': m_w_down[0].astype(BF16),
    }
    lam_p = b_lambda[0]
    subln_g = b_subln[0]

    n_mem = mem_prompt.shape[1]
    mem_kt, mem_vt = _memkv(mem_prompt, g_mem, w_mem_kv.astype(BF16))

    def mem_out(x):
        return jnp.transpose(x.reshape(depth, bp, MEM_HEADS, MEM_HEAD_DIM, n_mem), (0, 1, 4, 2, 3))

    def attend_prompt(q_bf, k, v, k_bf, v_bf):
        return _attn_prompt(q_bf, k_bf, v_bf, 512, lam_p, subln_g, lam_init)

    y_p, conv_p, k_p, v_p = _trunk(
        x_prompt, jnp.arange(tp), jnp.zeros((bp, CONV_WIDTH - 1, c), F32), mem_kt, mem_vt, attend_prompt, w,
        tm=256, tok_tm=256, moe_blk=256)

    past = page_table.shape[1] * cache_k.shape[1]
    pages = math.gcd(page_table.shape[1], 16)
    cache_kh = jnp.transpose(cache_k, (0, 2, 1, 3))
    cache_vh = jnp.transpose(cache_v, (0, 2, 1, 3))

    def mem_in(x):
        return jnp.transpose(x, (0, 1, 3, 4, 2)).reshape(depth, bs, MEM_W, x.shape[2])

    def attend_sample(q_bf, k, v, k_bf, v_bf):
        return _attn_sample(q_bf, k, v, cache_kh, cache_vh, page_table, pages, lam_p, subln_g, lam_init)

    y_s, conv_s, k_s, v_s = _trunk(
        x_sample, past + jnp.arange(ts), state_conv[0], mem_in(cache_mem_k), mem_in(cache_mem_v), attend_sample, w,
        tm=ts, tok_tm=bs * ts, moe_blk=128)

    return (y_p, y_s, conv_p[None], conv_s[None], _token_major(k_p), _token_major(v_p), _token_major(k_s),
            _token_major(v_s), mem_out(mem_kt), mem_out(mem_vt))
```

```python
import functools
import math

import jax
import jax.numpy as jnp
from jax import lax
from jax.experimental import pallas as pl
from jax.experimental.pallas import tpu as pltpu

F32 = jnp.float32
BF16 = jnp.bfloat16

EPS = 1e-6
MEM_HEADS = 4
MEM_HEAD_DIM = 64
MEM_W = MEM_HEADS * MEM_HEAD_DIM
HEAD_W = 128
HALF_W = HEAD_W // 2
ROPE_HALF = HALF_W // 2
ROPE_THETA = 10000.0
CONV_WIDTH = 3
TOP_K = 2
NEG = -1e30
QK_SCALE = HALF_W ** -0.5 * math.log2(math.e)
MEM_SCALE = MEM_HEAD_DIM ** -0.5
LANES = 128
SUBLANES = 8
VMEM_LIMIT = 56 * 1024 * 1024


def _cparams(*sem):
    return pltpu.CompilerParams(dimension_semantics=sem, vmem_limit_bytes=VMEM_LIMIT)


def _rms(x, g):
    return x * lax.rsqrt(jnp.mean(x * x, axis=-1, keepdims=True) + EPS) * g


def _dot(a, b):
    return jnp.dot(a, b, preferred_element_type=F32)


def _dot_t(a, b):
    return lax.dot_general(a, b, (((1,), (1,)), ((), ())), preferred_element_type=F32)


def _resident(shape):
    n = len(shape)
    return pl.BlockSpec(shape, lambda *_: (0,) * n, pipeline_mode=pl.Buffered(1))


def _mem_attn(qm, mk_t, mv_t):
    q = (qm * MEM_SCALE).astype(BF16)
    kb = mk_t.astype(BF16)
    vb = mv_t.astype(BF16)
    lane = lax.broadcasted_iota(jnp.int32, q.shape, 1)
    out = jnp.zeros(q.shape, F32)
    for h in range(MEM_HEADS):
        in_head = (lane >= h * MEM_HEAD_DIM) & (lane < (h + 1) * MEM_HEAD_DIM)
        s = _dot(jnp.where(in_head, q, jnp.zeros_like(q)), kb)
        m = jnp.max(s, axis=-1, keepdims=True)
        p = jnp.exp(s - m)
        p = p / jnp.sum(p, axis=-1, keepdims=True)
        out = jnp.where(in_head, _dot_t(p.astype(BF16), vb), out)
    return out


def _rope_slab(x, cos, sin_signed):
    lane = lax.broadcasted_iota(jnp.int32, x.shape, 1)
    first_half = (lane % HALF_W) < ROPE_HALF
    partner = jnp.where(first_half, pltpu.roll(x, HEAD_W - ROPE_HALF, 1), pltpu.roll(x, ROPE_HALF, 1))
    return x * cos + partner * sin_signed


def _lam(lam_ref, lam_init):
    lp = lam_ref[...]
    a = jnp.sum(lp[0:1] * lp[1:2], axis=-1, keepdims=True)
    b = jnp.sum(lp[2:3] * lp[3:4], axis=-1, keepdims=True)
    return jnp.exp(a) - jnp.exp(b) + lam_init


def _subln(o, g, lam_init):
    o = o * lax.rsqrt(jnp.mean(o * o, axis=-1, keepdims=True) + EPS)
    return o * g * (1.0 - lam_init)


def _memkv_kernel(x_ref, g_ref, w_ref, k_ref, v_ref):
    hn = _rms(x_ref[0], g_ref[0]).astype(BF16)
    kv_t = _dot(hn, w_ref[0]).T
    k_ref[0, 0] = kv_t[:MEM_W]
    v_ref[0, 0] = kv_t[MEM_W:]


def _memkv(mem, g_mem, w_bf):
    depth, d, n = w_bf.shape
    b, n_mem, _ = mem.shape
    out = jax.ShapeDtypeStruct((depth, b, MEM_W, n_mem), F32)
    return pl.pallas_call(
        _memkv_kernel,
        grid=(depth, b),
        in_specs=[
            pl.BlockSpec((1, n_mem, d), lambda l, bi: (bi, 0, 0)),
            pl.BlockSpec((1, 1, d), lambda l, bi: (l, 0, 0)),
            pl.BlockSpec((1, d, n), lambda l, bi: (l, 0, 0)),
        ],
        out_specs=[pl.BlockSpec((1, 1, MEM_W, n_mem), lambda l, bi: (l, bi, 0, 0))] * 2,
        out_shape=[out, out],
        compiler_params=_cparams("arbitrary", "arbitrary"),
        name="memkv",
    )(mem, g_mem.reshape(depth, 1, d), w_bf)


def _mixa_kernel(x_ref, g_ref, win_ref, cw_ref, wout_ref, mk_ref, mv_ref, prev_ref, h_ref, st_ref, ubuf,
                 *, tm, c, n_t):
    i = pl.program_id(1)
    x = x_ref[...]
    hn = _rms(x, g_ref[...]).astype(BF16)
    proj = _dot(hn, win_ref[...])
    gate_b = proj[:, :c]
    u = proj[:, c:2 * c] * proj[:, 2 * c:3 * c]
    qm = proj[:, 3 * c:]

    @pl.when(i == 0)
    def _():
        ubuf[SUBLANES - 2:SUBLANES, :] = prev_ref[0]

    @pl.when(i > 0)
    def _():
        ubuf[SUBLANES - 2:SUBLANES, :] = ubuf[tm + SUBLANES - 2:tm + SUBLANES, :]

    ubuf[SUBLANES:SUBLANES + tm, :] = u
    cw = cw_ref[...]
    y = cw[0:1] * ubuf[SUBLANES - 2:SUBLANES - 2 + tm, :]
    y = y + cw[1:2] * ubuf[SUBLANES - 1:SUBLANES - 1 + tm, :]
    y = y + cw[2:3] * u
    mix = gate_b * y
    mo = _mem_attn(qm, mk_ref[0], mv_ref[0])
    out = _dot(mix.astype(BF16), wout_ref[:c, :]) + _dot(mo.astype(BF16), wout_ref[c:, :])
    h_ref[...] = x + out

    @pl.when(i == n_t - 1)
    def _():
        st_ref[0] = ubuf[tm + SUBLANES - 2:tm + SUBLANES, :]


def _mem_spec(n_mem):
    return pl.BlockSpec((1, MEM_W, n_mem), lambda bi, i: (bi, 0, 0))


def _mixa(x2d, b, t, tm, g, win_bf, conv_w, wout_bf, mem_k, mem_v, prev):
    d = x2d.shape[1]
    c = d - MEM_W
    n_t = t // tm
    n_mem = mem_k.shape[2]
    kern = functools.partial(_mixa_kernel, tm=tm, c=c, n_t=n_t)
    return pl.pallas_call(
        kern,
        grid=(b, n_t),
        in_specs=[
            pl.BlockSpec((tm, d), lambda bi, i: (bi * n_t + i, 0)),
            _resident((1, d)),
            _resident(win_bf.shape),
            _resident(conv_w.shape),
            _resident(wout_bf.shape),
            _mem_spec(n_mem),
            _mem_spec(n_mem),
            pl.BlockSpec((1, CONV_WIDTH - 1, c), lambda bi, i: (bi, 0, 0)),
        ],
        out_specs=[
            pl.BlockSpec((tm, d), lambda bi, i: (bi * n_t + i, 0)),
            pl.BlockSpec((1, CONV_WIDTH - 1, c), lambda bi, i: (bi, 0, 0)),
        ],
        out_shape=[
            jax.ShapeDtypeStruct(x2d.shape, F32),
            jax.ShapeDtypeStruct((b, CONV_WIDTH - 1, c), F32),
        ],
        scratch_shapes=[pltpu.VMEM((tm + SUBLANES, c), F32)],
        compiler_params=_cparams("arbitrary", "arbitrary"),
        name="mixa",
    )(x2d, g.reshape(1, d), win_bf, conv_w, wout_bf, mem_k, mem_v, prev)


def _silu(g):
    return g / (1.0 + jnp.exp(-g))


def _ffn_kernel(h_ref, g_ref, wg_ref, wu_ref, wd_ref, o_ref):
    h = h_ref[...]
    hn = _rms(h, g_ref[...]).astype(BF16)
    a = _silu(_dot(hn, wg_ref[...])) * _dot(hn, wu_ref[...])
    o_ref[...] = h + _dot(a.astype(BF16), wd_ref[...])


def _ffn(h2d, tm, g, wg_bf, wu_bf, wd_bf):
    m, d = h2d.shape
    return pl.pallas_call(
        _ffn_kernel,
        grid=(m // tm,),
        in_specs=[
            pl.BlockSpec((tm, d), lambda i: (i, 0)),
            _resident((1, d)),
            _resident(wg_bf.shape),
            _resident(wu_bf.shape),
            _resident(wd_bf.shape),
        ],
        out_specs=pl.BlockSpec((tm, d), lambda i: (i, 0)),
        out_shape=jax.ShapeDtypeStruct(h2d.shape, F32),
        compiler_params=_cparams("arbitrary"),
        name="ffn",
    )(h2d, g.reshape(1, d), wg_bf, wu_bf, wd_bf)


def _kvq_kernel(h_ref, gkv_ref, gq_ref, wkv_ref, wq_ref, cos_ref, sin_ref,
                k_ref, v_ref, kb_ref, vb_ref, q_ref, qm_ref, *, n_heads):
    x = h_ref[...]
    xn = x * lax.rsqrt(jnp.mean(x * x, axis=-1, keepdims=True) + EPS)
    cos = cos_ref[...]
    sin = sin_ref[...]
    kv = _dot((xn * gkv_ref[...]).astype(BF16), wkv_ref[...])
    for h in range(n_heads):
        r = _rope_slab(kv[:, h * HEAD_W:(h + 1) * HEAD_W], cos, sin)
        k_ref[0, h] = r
        kb_ref[0, h] = r.astype(BF16)
        v = kv[:, (n_heads + h) * HEAD_W:(n_heads + h + 1) * HEAD_W]
        v_ref[0, h] = v
        vb_ref[0, h] = v.astype(BF16)
    p = _dot((xn * gq_ref[...]).astype(BF16), wq_ref[...])
    for h in range(n_heads):
        q_ref[0, h] = (_rope_slab(p[:, h * HEAD_W:(h + 1) * HEAD_W], cos, sin) * QK_SCALE).astype(BF16)
    qm_ref[...] = p[:, n_heads * HEAD_W:]


def _rope_tables(pos):
    inv = ROPE_THETA ** (-jnp.arange(ROPE_HALF, dtype=F32) / ROPE_HALF)
    ang = pos.astype(F32)[:, None] * inv[None, :]
    cos = jnp.tile(jnp.cos(ang), (1, HEAD_W // ROPE_HALF))
    sin = jnp.sin(ang)
    sin = jnp.tile(jnp.concatenate([-sin, sin], axis=-1), (1, HEAD_W // HALF_W))
    return cos, sin


def _kvqproj(h2d, b, t, tm, g_kv, wkv_bf, g_q, wq_bf, cos, sin):
    m, d = h2d.shape
    n_heads = wkv_bf.shape[1] // (2 * HEAD_W)
    n_t = t // tm
    kern = functools.partial(_kvq_kernel, n_heads=n_heads)
    head_major = pl.BlockSpec((1, n_heads, tm, HEAD_W), lambda bi, i: (bi, 0, i, 0))
    rope = pl.BlockSpec((tm, HEAD_W), lambda bi, i: (i, 0))
    rows = lambda bi, i: (bi * n_t + i, 0)
    hm_f32 = jax.ShapeDtypeStruct((b, n_heads, t, HEAD_W), F32)
    hm_bf16 = jax.ShapeDtypeStruct((b, n_heads, t, HEAD_W), BF16)
    return pl.pallas_call(
        kern,
        grid=(b, n_t),
        in_specs=[pl.BlockSpec((tm, d), rows), _resident((1, d)), _resident((1, d)), _resident(wkv_bf.shape),
                  _resident(wq_bf.shape), rope, rope],
        out_specs=[head_major] * 5 + [pl.BlockSpec((tm, MEM_W), rows)],
        out_shape=[hm_f32, hm_f32, hm_bf16, hm_bf16, hm_bf16, jax.ShapeDtypeStruct((m, MEM_W), F32)],
        compiler_params=_cparams("arbitrary", "arbitrary"),
        name="kvqproj",
    )(h2d, g_kv.reshape(1, d), g_q.reshape(1, d), wkv_bf, wq_bf, cos, sin)


def _stack_components(q):
    lane = lax.broadcasted_iota(jnp.int32, q.shape, 1)
    zero = jnp.zeros_like(q)
    return jnp.concatenate([jnp.where(lane < HALF_W, q, zero), jnp.where(lane >= HALF_W, q, zero)], axis=0)


def _online_update(s, pv, m_ref, l_ref, acc_ref):
    cols = [s[:, c * LANES:(c + 1) * LANES] for c in range(s.shape[1] // LANES)]
    m_prev = m_ref[...]
    m_new = jnp.maximum(m_prev, jnp.max(functools.reduce(jnp.maximum, cols), axis=-1, keepdims=True))
    alpha = jnp.exp2(m_prev - m_new)
    ps = [jnp.exp2(c - m_new) for c in cols]
    l_ref[...] = alpha * l_ref[...] + functools.reduce(jnp.add, ps)
    acc_ref[...] = alpha * acc_ref[...] + pv(jnp.concatenate(ps, axis=1).astype(BF16))
    m_ref[...] = m_new


def _init_online(m_ref, l_ref, acc_ref):
    m_ref[...] = jnp.full(m_ref.shape, NEG, F32)
    l_ref[...] = jnp.zeros(l_ref.shape, F32)
    acc_ref[...] = jnp.zeros(acc_ref.shape, F32)


def _diff_combine(l_ref, acc_ref, t, lam):
    o = acc_ref[...] / jnp.sum(l_ref[...], axis=-1, keepdims=True)
    return o[:t] - lam * o[t:]


def _attn_prompt_kernel(q_ref, k_ref, v_ref, lam_ref, sg_ref, o_ref, m_ref, l_ref, acc_ref, *, tq, lam_init):
    qi = pl.program_id(2)
    qs = _stack_components(q_ref[0, 0])
    _init_online(m_ref, l_ref, acc_ref)

    def chunk(j, masked):
        start = pl.multiple_of(j * tq, tq)
        s = _dot_t(qs, k_ref[0, 0, pl.ds(start, tq), :])
        if masked:
            row = lax.broadcasted_iota(jnp.int32, s.shape, 0) % tq
            col = lax.broadcasted_iota(jnp.int32, s.shape, 1)
            s = jnp.where(col <= row, s, NEG)
        v = v_ref[0, 0, pl.ds(start, tq), :]
        _online_update(s, lambda p: _dot(p, v), m_ref, l_ref, acc_ref)

    def body(j, carry):
        chunk(2 * j, False)
        chunk(2 * j + 1, False)
        return carry

    lax.fori_loop(0, qi // 2, body, 0)

    @pl.when(qi % 2 == 1)
    def _():
        chunk(qi - 1, False)

    chunk(qi, True)
    o = _diff_combine(l_ref, acc_ref, tq, _lam(lam_ref, lam_init))
    o_ref[...] = _subln(o, sg_ref[...], lam_init).astype(BF16)


def _attn_prompt(q_bf, k_bf, v_bf, tq, lam_p, subln_g, lam_init):
    b, n_heads, t, _ = q_bf.shape
    nq = t // tq
    kern = functools.partial(_attn_prompt_kernel, tq=tq, lam_init=lam_init)
    seq = pl.BlockSpec((1, 1, t, HEAD_W), lambda bi, h, qi: (bi, h, 0, 0))
    return pl.pallas_call(
        kern,
        grid=(b, n_heads, nq),
        in_specs=[
            pl.BlockSpec((1, 1, tq, HEAD_W), lambda bi, h, qi: (bi, h, qi, 0)),
            seq,
            seq,
            pl.BlockSpec(lam_p.shape, lambda bi, h, qi: (0, 0)),
            pl.BlockSpec((1, HEAD_W), lambda bi, h, qi: (0, 0)),
        ],
        out_specs=pl.BlockSpec((tq, HEAD_W), lambda bi, h, qi: (bi * nq + qi, h)),
        out_shape=jax.ShapeDtypeStruct((b * t, n_heads * HEAD_W), BF16),
        scratch_shapes=[pltpu.VMEM((2 * tq, LANES), F32)] * 3,
        compiler_params=_cparams("arbitrary", "arbitrary", "arbitrary"),
        name="attn_prompt",
    )(q_bf, k_bf, v_bf, lam_p, subln_g.reshape(1, HEAD_W))


def _attn_sample_kernel(pt_ref, q_ref, kn_ref, vn_ref, *rest, ts, n_heads, pages, n_groups, lam_init):
    k_refs = rest[:pages]
    v_refs = rest[pages:2 * pages]
    lam_ref, sg_ref, o_ref, qs_ref, m_ref, l_ref, acc_ref = rest[2 * pages:]
    g = pl.program_id(1)

    r = 2 * ts

    @pl.when(g == 0)
    def _():
        for h in range(n_heads):
            qs_ref[h] = _stack_components(q_ref[0, h])
        _init_online(m_ref, l_ref, acc_ref)

    def all_heads(keys, values):
        s = jnp.concatenate([_dot_t(qs_ref[h], keys(h)) for h in range(n_heads)], axis=0)

        def pv(p):
            return jnp.concatenate([_dot(p[h * r:(h + 1) * r], values(h)) for h in range(n_heads)], axis=0)

        return s, pv

    def past(refs):
        return lambda h: jnp.concatenate([x[0, h].astype(BF16) for x in refs], axis=0)

    s, pv = all_heads(past(k_refs), past(v_refs))
    _online_update(s, pv, m_ref, l_ref, acc_ref)

    @pl.when(g == n_groups - 1)
    def _():
        pad = jnp.zeros((LANES - ts, HEAD_W), BF16)

        def new(ref):
            return lambda h: jnp.concatenate([ref[0, h].astype(BF16), pad], axis=0)

        s, pv = all_heads(new(kn_ref), new(vn_ref))
        row = lax.broadcasted_iota(jnp.int32, s.shape, 0) % ts
        col = lax.broadcasted_iota(jnp.int32, s.shape, 1)
        _online_update(jnp.where(col <= row, s, NEG), pv, m_ref, l_ref, acc_ref)
        lam = _lam(lam_ref, lam_init)
        for h in range(n_heads):
            rows = pl.ds(h * r, r)
            o = _diff_combine(l_ref.at[rows], acc_ref.at[rows], ts, lam)
            o_ref[:, h * HEAD_W:(h + 1) * HEAD_W] = _subln(o, sg_ref[...], lam_init).astype(BF16)


def _attn_sample(q_bf, k_new, v_new, cache_k, cache_v, page_table, pages, lam_p, subln_g, lam_init):
    bs, n_heads, ts, _ = q_bf.shape
    ps = cache_k.shape[2]
    n_groups = page_table.shape[1] // pages
    kern = functools.partial(_attn_sample_kernel, ts=ts, n_heads=n_heads, pages=pages, n_groups=n_groups,
                             lam_init=lam_init)

    def page_spec(i):
        return pl.BlockSpec((1, n_heads, ps, HEAD_W), lambda bi, g, pt: (pt[bi, g * pages + i], 0, 0, 0))

    new_rows = pl.BlockSpec((1, n_heads, ts, HEAD_W), lambda bi, g, pt: (bi, 0, 0, 0))
    grid_spec = pltpu.PrefetchScalarGridSpec(
        num_scalar_prefetch=1,
        grid=(bs, n_groups),
        in_specs=[new_rows] * 3
        + [page_spec(i) for i in range(pages)] * 2
        + [pl.BlockSpec(lam_p.shape, lambda bi, g, pt: (0, 0)), pl.BlockSpec((1, HEAD_W), lambda bi, g, pt: (0, 0))],
        out_specs=pl.BlockSpec((ts, n_heads * HEAD_W), lambda bi, g, pt: (bi, 0)),
        scratch_shapes=[pltpu.VMEM((n_heads, 2 * ts, HEAD_W), BF16)]
        + [pltpu.VMEM((n_heads * 2 * ts, LANES), F32)] * 3,
    )
    return pl.pallas_call(
        kern,
        grid_spec=grid_spec,
        out_shape=jax.ShapeDtypeStruct((bs * ts, n_heads * HEAD_W), BF16),
        compiler_params=_cparams("arbitrary", "arbitrary"),
        name="attn_sample",
    )(page_table, q_bf, k_new, v_new, *([cache_k] * pages), *([cache_v] * pages), lam_p, subln_g.reshape(1, HEAD_W))


def _route(h, g, r_hi, r_lo, base, n_experts):
    hn = _rms(h, g)
    hi = hn.astype(BF16)
    lo = (hn - hi.astype(F32)).astype(BF16)
    logits = _dot(hi, r_hi) + (_dot(hi, r_lo) + _dot(lo, r_hi))
    lane = lax.broadcasted_iota(jnp.int32, logits.shape, 1)
    logits = jnp.where(lane < n_experts, logits, -jnp.inf)
    v1 = jnp.max(logits, axis=-1, keepdims=True)
    e1 = jnp.min(jnp.where(logits == v1, lane, LANES), axis=-1, keepdims=True)
    rest = jnp.where(lane == e1, -jnp.inf, logits)
    v2 = jnp.max(rest, axis=-1, keepdims=True)
    e2 = jnp.min(jnp.where(rest == v2, lane, LANES), axis=-1, keepdims=True)
    ex = jnp.exp(v2 - v1)
    g1 = 1.0 / (1.0 + ex)
    g2 = ex / (1.0 + ex)
    sel = jnp.where((lane == e1) | (lane == e2), 1.0, 0.0)
    t = h.shape[0]
    earlier = lax.broadcasted_iota(jnp.int32, (t, t), 0) > lax.broadcasted_iota(jnp.int32, (t, t), 1)
    cum = _dot(jnp.where(earlier, 1.0, 0.0).astype(BF16), sel.astype(BF16)) + base
    r1 = jnp.sum(jnp.where(lane == e1, cum, 0.0), axis=-1, keepdims=True)
    r2 = jnp.sum(jnp.where(lane == e2, cum, 0.0), axis=-1, keepdims=True)
    out = jnp.where(lane == 0, e1.astype(F32), 0.0)
    for i, val in enumerate((e2.astype(F32), g1, g2, r1, r2), start=1):
        out = jnp.where(lane == i, val, out)
    return out, sel


def _mixb_kernel(mix_ref, qm_ref, x_ref, wout_ref, mk_ref, mv_ref, gf_ref, rhi_ref, rlo_ref,
                 h_ref, route_ref, cnt_ref, carry, *, n_experts):
    @pl.when((pl.program_id(0) == 0) & (pl.program_id(1) == 0))
    def _():
        carry[...] = jnp.zeros(carry.shape, F32)

    c = mix_ref.shape[1]
    mo = _mem_attn(qm_ref[...], mk_ref[0], mv_ref[0])
    h = x_ref[...] + (_dot(mix_ref[...], wout_ref[:c, :]) + _dot(mo.astype(BF16), wout_ref[c:, :]))
    h_ref[...] = h
    route, sel = _route(h, gf_ref[...], rhi_ref[...], rlo_ref[...], carry[...], n_experts)
    route_ref[...] = route
    carry[...] += jnp.sum(sel, axis=0, keepdims=True)
    cnt_ref[...] = carry[...]


def _mixb(mix_bf, qm, x2d, b, t, tm, wout_bf, mem_k, mem_v, g_ffn, router):
    m, d = x2d.shape
    c = mix_bf.shape[1]
    n_t = t // tm
    n_mem = mem_k.shape[2]
    n_experts = router.shape[1]
    r_pad = jnp.zeros((d, LANES), F32).at[:, :n_experts].set(router)
    r_hi = r_pad.astype(BF16)
    r_lo = (r_pad - r_hi.astype(F32)).astype(BF16)
    row = lambda bi, i: (bi * n_t + i, 0)
    kern = functools.partial(_mixb_kernel, n_experts=n_experts)
    return pl.pallas_call(
        kern,
        grid=(b, n_t),
        in_specs=[
            pl.BlockSpec((tm, c), row),
            pl.BlockSpec((tm, MEM_W), row),
            pl.BlockSpec((tm, d), row),
            _resident(wout_bf.shape),
            _mem_spec(n_mem),
            _mem_spec(n_mem),
            _resident((1, d)),
            _resident((d, LANES)),
            _resident((d, LANES)),
        ],
        out_specs=[
            pl.BlockSpec((tm, d), row),
            pl.BlockSpec((tm, LANES), row),
            pl.BlockSpec((1, LANES), lambda bi, i: (0, 0)),
        ],
        out_shape=[
            jax.ShapeDtypeStruct((m, d), F32),
            jax.ShapeDtypeStruct((m, LANES), F32),
            jax.ShapeDtypeStruct((1, LANES), F32),
        ],
        scratch_shapes=[pltpu.VMEM((1, LANES), F32)],
        compiler_params=_cparams("arbitrary", "arbitrary"),
        name="mixb",
    )(mix_bf, qm, x2d, wout_bf, mem_k, mem_v, g_ffn.reshape(1, d), r_hi, r_lo)


GATHER_UNROLL = 8


def _row_copy(src_hbm, row, dst, slot, sem):
    return pltpu.make_async_copy(src_hbm.at[pl.ds(row, 1)], dst.at[pl.ds(slot, 1)], sem)


def _gather_start(src_hbm, idx_ref, base, dst, sem, n):
    def issue(j, carry):
        for u in range(GATHER_UNROLL):
            r = j * GATHER_UNROLL + u
            _row_copy(src_hbm, idx_ref[base + r], dst, r, sem).start(priority=u % 2)
        return carry

    lax.fori_loop(0, n // GATHER_UNROLL, issue, 0)


def _gather_start_inline(src_hbm, idx_ref, base, dst, sem, n):
    for r in range(n):
        _row_copy(src_hbm, idx_ref[base + r], dst, r, sem).start(priority=r % 2)


def _gather_wait(src_hbm, dst, sem, n):
    def wait(r, carry):
        _row_copy(src_hbm, 0, dst, r, sem).wait()
        return carry

    lax.fori_loop(0, n, wait, 0, unroll=GATHER_UNROLL)


def _expert_kernel(blk_e_ref, tok_ref, nused_ref, h_hbm, g_ref, wg_ref, wu_ref, wd_ref, o_ref, xbuf, sem, *, blk):
    i = pl.program_id(0)
    n_used = nused_ref[0]
    slot = i % 2

    @pl.when(i == 0)
    def _():
        _gather_start(h_hbm, tok_ref, 0, xbuf.at[0], sem.at[0], blk)

    @pl.when(i + 1 < n_used)
    def _():
        _gather_start(h_hbm, tok_ref, (i + 1) * blk, xbuf.at[1 - slot], sem.at[1 - slot], blk)

    @pl.when(i < n_used)
    def _():
        _gather_wait(h_hbm, xbuf.at[slot], sem.at[slot], blk)
        hn = _rms(xbuf[slot], g_ref[...]).astype(BF16)
        a = _silu(_dot(hn, wg_ref[0])) * _dot(hn, wu_ref[0])
        o_ref[...] = _dot(a.astype(BF16), wd_ref[0])

    @pl.when(i >= n_used)
    def _():
        o_ref[...] = jnp.zeros(o_ref.shape, F32)


def _experts(h2d, g, wg_bf, wu_bf, wd_bf, blk_e, buf_tok, n_used, blk):
    n_rows = buf_tok.shape[0]
    n_blk = n_rows // blk
    _, d, ff = wg_bf.shape
    kern = functools.partial(_expert_kernel, blk=blk)
    grid_spec = pltpu.PrefetchScalarGridSpec(
        num_scalar_prefetch=3,
        grid=(n_blk,),
        in_specs=[
            pl.BlockSpec(memory_space=pl.ANY),
            pl.BlockSpec((1, d), lambda i, be, tk, nu: (0, 0)),
            pl.BlockSpec((1, d, ff), lambda i, be, tk, nu: (be[i], 0, 0)),
            pl.BlockSpec((1, d, ff), lambda i, be, tk, nu: (be[i], 0, 0)),
            pl.BlockSpec((1, ff, d), lambda i, be, tk, nu: (be[i], 0, 0)),
        ],
        out_specs=pl.BlockSpec((blk, d), lambda i, be, tk, nu: (i, 0)),
        scratch_shapes=[pltpu.VMEM((2, blk, d), F32), pltpu.SemaphoreType.DMA((2,))],
    )
    return pl.pallas_call(
        kern,
        grid_spec=grid_spec,
        out_shape=jax.ShapeDtypeStruct((n_rows, d), F32),
        compiler_params=_cparams("arbitrary"),
        name="experts",
    )(blk_e, buf_tok, n_used, h2d, g.reshape(1, d), wg_bf, wu_bf, wd_bf)


def _combine_kernel(pos_ref, h_ref, route_ref, g_ref, yb_hbm, o_ref, ybuf, sem, *, tc, n_steps):
    i = pl.program_id(0)
    slot = i % 2

    @pl.when(i == 0)
    def _():
        for k in range(TOP_K):
            _gather_start(yb_hbm, pos_ref, k * n_steps * tc, ybuf.at[0, k], sem.at[0], tc)

    for k in range(TOP_K):
        _gather_wait(yb_hbm, ybuf.at[slot, k], sem.at[slot], tc)
    nxt = jnp.minimum(i + 1, n_steps - 1)
    for k in range(TOP_K):
        _gather_start_inline(yb_hbm, pos_ref, (k * n_steps + nxt) * tc, ybuf.at[1 - slot, k], sem.at[1 - slot], tc)
    route = route_ref[...]
    y = ybuf[slot, 0] * route[:, 2:3] + ybuf[slot, 1] * route[:, 3:4]
    o_ref[...] = _rms(h_ref[...] + y, g_ref[...])

    @pl.when(i == n_steps - 1)
    def _():
        for k in range(TOP_K):
            _gather_wait(yb_hbm, ybuf.at[1 - slot, k], sem.at[1 - slot], tc)


def _combine(h2d, route, yb, pos_k_major, g, tc):
    m, d = h2d.shape
    n_steps = m // tc
    kern = functools.partial(_combine_kernel, tc=tc, n_steps=n_steps)
    grid_spec = pltpu.PrefetchScalarGridSpec(
        num_scalar_prefetch=1,
        grid=(n_steps,),
        in_specs=[
            pl.BlockSpec((tc, d), lambda i, p: (i, 0)),
            pl.BlockSpec((tc, LANES), lambda i, p: (i, 0)),
            pl.BlockSpec((1, d), lambda i, p: (0, 0)),
            pl.BlockSpec(memory_space=pl.ANY),
        ],
        out_specs=pl.BlockSpec((tc, d), lambda i, p: (i, 0)),
        scratch_shapes=[pltpu.VMEM((2, TOP_K, tc, d), F32), pltpu.SemaphoreType.DMA((2,))],
    )
    return pl.pallas_call(
        kern,
        grid_spec=grid_spec,
        out_shape=jax.ShapeDtypeStruct((m, d), F32),
        compiler_params=_cparams("arbitrary"),
        name="combine",
    )(pos_k_major, h2d, route, g.reshape(1, d), yb)


def _slot_tokens_kernel(dest_ref, o_ref, *, n_assign, n_rows):
    def zero(p, carry):
        o_ref[p] = 0
        return carry

    lax.fori_loop(0, n_rows, zero, 0, unroll=GATHER_UNROLL)

    def put(j, carry):
        for u in range(GATHER_UNROLL):
            tok = j * GATHER_UNROLL + u
            for k in range(TOP_K):
                o_ref[dest_ref[tok * TOP_K + k]] = tok
        return carry

    lax.fori_loop(0, n_assign // (TOP_K * GATHER_UNROLL), put, 0)


def _slot_tokens(dest_flat, n_rows):
    n_assign = dest_flat.shape[0]
    kern = functools.partial(_slot_tokens_kernel, n_assign=n_assign, n_rows=n_rows)
    grid_spec = pltpu.PrefetchScalarGridSpec(
        num_scalar_prefetch=1, grid=(1,), in_specs=[], out_specs=pl.BlockSpec(memory_space=pltpu.SMEM))
    return pl.pallas_call(
        kern,
        grid_spec=grid_spec,
        out_shape=jax.ShapeDtypeStruct((n_rows,), jnp.int32),
        compiler_params=_cparams("arbitrary"),
        name="slot_tokens",
    )(dest_flat)


def _moe(h2d, route, counts, tc, g_ffn, wg_bf, wu_bf, wd_bf, g_final, blk):
    m, d = h2d.shape
    n_experts = wg_bf.shape[0]
    a = m * TOP_K
    counts = counts[0, :n_experts].astype(jnp.int32)
    padded = (counts + blk - 1) // blk * blk
    cum_pad = jnp.cumsum(padded)
    top_e = route[:, :TOP_K].astype(jnp.int32)
    dest = (cum_pad - padded)[top_e] + route[:, 2 * TOP_K:3 * TOP_K].astype(jnp.int32)
    n_blk = -(-(a + n_experts * (blk - 1)) // blk)
    buf_tok = _slot_tokens(dest.reshape(-1), n_blk * blk)
    n_used = (cum_pad[-1] // blk).astype(jnp.int32)
    blk_i = jnp.minimum(jnp.arange(n_blk, dtype=jnp.int32), n_used - 1)
    blk_e = jnp.sum((blk_i[:, None] * blk >= cum_pad[None, :]).astype(jnp.int32), axis=1)
    blk_e = jnp.minimum(blk_e, n_experts - 1)
    yb = _experts(h2d, g_ffn, wg_bf, wu_bf, wd_bf, blk_e, buf_tok, n_used.reshape(1), blk)
    return _combine(h2d, route, yb, dest.T.reshape(-1), g_final, tc)


def _head_major(x, b, t):
    n_heads = x.shape[1]
    return jnp.transpose(x.reshape(n_heads, b, t, HEAD_W), (1, 0, 2, 3))


def _trunk(x, pos, conv_prev, mem_k, mem_v, attend, w, tm_a, tm, tok_tm, moe_blk):
    b, t, d = x.shape
    x2d = x.reshape(b * t, d)
    h, conv_st = _mixa(x2d, b, t, tm_a, w['g_mix'][0], w['a_w_in'], w['a_conv'], w['a_w_out'], mem_k[0], mem_v[0],
                       conv_prev)
    h = _ffn(h, tok_tm, w['g_ffn'][0], w['f_w_gate'], w['f_w_up'], w['f_w_down'])
    cos, sin = _rope_tables(pos)
    proj = (w['g_kv'], w['w_kv'], w['g_mix'][1], w['b_w_in'])
    if tok_tm > t:
        cos, sin = jnp.tile(cos, (b, 1)), jnp.tile(sin, (b, 1))
        *heads, qm = _kvqproj(h, 1, b * t, tok_tm, *proj, cos, sin)
        k, v, k_bf, v_bf, q_bf = (_head_major(o, b, t) for o in heads)
    else:
        k, v, k_bf, v_bf, q_bf, qm = _kvqproj(h, b, t, tok_tm, *proj, cos, sin)
    mix = attend(q_bf, k, v, k_bf, v_bf)
    h, route, counts = _mixb(mix, qm, h, b, t, tm, w['b_w_out'], mem_k[1], mem_v[1], w['g_ffn'][1], w['m_router'])
    y = _moe(h, route, counts, tok_tm, w['g_ffn'][1], w['m_w_gate'], w['m_w_up'], w['m_w_down'], w['g_final'],
             moe_blk)
    return y.reshape(b, t, d), conv_st, k, v


def _token_major(x):
    return jnp.transpose(x, (0, 2, 1, 3))


def kernel(x_prompt, x_sample, state_conv, cache_k, cache_v, cache_mem_k, cache_mem_v, page_table, mem_prompt, g_mix, g_ffn, g_mem, w_mem_kv, a_w_in, a_conv, a_w_out, g_kv, w_kv, b_w_in, b_lambda, b_subln, b_w_out, f_w_gate, f_w_up, f_w_down, m_router, m_w_gate, m_w_up, m_w_down, g_final):
    bp, tp, d = x_prompt.shape
    bs, ts, _ = x_sample.shape
    depth = g_mix.shape[0]
    n_a = a_w_in.shape[0]
    assert depth == 2 and n_a == 1 and b_w_in.shape[0] == 1, "one conv layer followed by one attention layer"
    assert ts == SUBLANES and tp % 512 == 0
    c = d - MEM_W
    lam_init = 0.8 - 0.6 * math.exp(-0.3 * n_a)
    w = {
        'g_mix': g_mix, 'g_ffn': g_ffn, 'g_kv': g_kv, 'g_final': g_final,
        'a_w_in': a_w_in[0].astype(BF16), 'a_conv': a_conv[0], 'a_w_out': a_w_out[0].astype(BF16),
        'w_kv': w_kv.astype(BF16), 'b_w_in': b_w_in[0].astype(BF16), 'b_w_out': b_w_out[0].astype(BF16),
        'f_w_gate': f_w_gate[0].astype(BF16), 'f_w_up': f_w_up[0].astype(BF16), 'f_w_down': f_w_down[0].astype(BF16),
        'm_router': m_router[0], 'm_w_gate': m_w_gate[0].astype(BF16), 'm_w_up': m_w_up[0].astype(BF16),
        'm_w_down': m_w_down[0].astype(BF16),
    }
    lam_p = b_lambda[0]
    subln_g = b_subln[0]

    n_mem = mem_prompt.shape[1]
    mem_kt, mem_vt = _memkv(mem_prompt, g_mem, w_mem_kv.astype(BF16))

    def mem_out(x):
        return jnp.transpose(x.reshape(depth, bp, MEM_HEADS, MEM_HEAD_DIM, n_mem), (0, 1, 4, 2, 3))

    def attend_prompt(q_bf, k, v, k_bf, v_bf):
        return _attn_prompt(q_bf, k_bf, v_bf, 512, lam_p, subln_g, lam_init)

    y_p, conv_p, k_p, v_p = _trunk(
        x_prompt, jnp.arange(tp), jnp.zeros((bp, CONV_WIDTH - 1, c), F32), mem_kt, mem_vt, attend_prompt, w,
        tm_a=512, tm=256, tok_tm=256, moe_blk=256)

    past = page_table.shape[1] * cache_k.shape[1]
    pages = math.gcd(page_table.shape[1], 16)
    cache_kh = jnp.transpose(cache_k, (0, 2, 1, 3))
    cache_vh = jnp.transpose(cache_v, (0, 2, 1, 3))

    def mem_in(x):
        return jnp.transpose(x, (0, 1, 3, 4, 2)).reshape(depth, bs, MEM_W, x.shape[2])

    def attend_sample(q_bf, k, v, k_bf, v_bf):
        return _attn_sample(q_bf, k, v, cache_kh, cache_vh, page_table, pages, lam_p, subln_g, lam_init)

    y_s, conv_s, k_s, v_s = _trunk(
        x_sample, past + jnp.arange(ts), state_conv[0], mem_in(cache_mem_k), mem_in(cache_mem_v), attend_sample, w,
        tm_a=ts, tm=ts, tok_tm=bs * ts, moe_blk=128)

    return (y_p, y_s, conv_p[None], conv_s[None], _token_major(k_p), _token_major(v_p), _token_major(k_s),
            _token_major(v_s), mem_out(mem_kt), mem_out(mem_vt))
```

```python
import functools
import math

import jax
import jax.numpy as jnp
from jax import lax
from jax.experimental import pallas as pl
from jax.experimental.pallas import tpu as pltpu

F32 = jnp.float32
BF16 = jnp.bfloat16

EPS = 1e-6
MEM_HEADS = 4
MEM_HEAD_DIM = 64
MEM_W = MEM_HEADS * MEM_HEAD_DIM
HEAD_W = 128
HALF_W = HEAD_W // 2
ROPE_HALF = HALF_W // 2
ROPE_THETA = 10000.0
CONV_WIDTH = 3
TOP_K = 2
NEG = -1e30
QK_SCALE = HALF_W ** -0.5 * math.log2(math.e)
MEM_SCALE = MEM_HEAD_DIM ** -0.5
LANES = 128
SUBLANES = 8
VMEM_LIMIT = 56 * 1024 * 1024


def _cparams(*sem):
    return pltpu.CompilerParams(dimension_semantics=sem, vmem_limit_bytes=VMEM_LIMIT)


def _rms(x, g):
    return x * lax.rsqrt(jnp.mean(x * x, axis=-1, keepdims=True) + EPS) * g


def _dot(a, b):
    return jnp.dot(a, b, preferred_element_type=F32)


def _dot_t(a, b):
    return lax.dot_general(a, b, (((1,), (1,)), ((), ())), preferred_element_type=F32)


def _resident(shape):
    n = len(shape)
    return pl.BlockSpec(shape, lambda *_: (0,) * n, pipeline_mode=pl.Buffered(1))


def _mem_attn(qm, mk_t, mv_t):
    q = (qm * MEM_SCALE).astype(BF16)
    kb = mk_t.astype(BF16)
    vb = mv_t.astype(BF16)
    lane = lax.broadcasted_iota(jnp.int32, q.shape, 1)
    out = jnp.zeros(q.shape, F32)
    for h in range(MEM_HEADS):
        in_head = (lane >= h * MEM_HEAD_DIM) & (lane < (h + 1) * MEM_HEAD_DIM)
        s = _dot(jnp.where(in_head, q, jnp.zeros_like(q)), kb)
        m = jnp.max(s, axis=-1, keepdims=True)
        p = jnp.exp(s - m)
        p = p / jnp.sum(p, axis=-1, keepdims=True)
        out = jnp.where(in_head, _dot_t(p.astype(BF16), vb), out)
    return out


def _rope_slab(x, cos, sin_signed):
    lane = lax.broadcasted_iota(jnp.int32, x.shape, 1)
    first_half = (lane % HALF_W) < ROPE_HALF
    partner = jnp.where(first_half, pltpu.roll(x, HEAD_W - ROPE_HALF, 1), pltpu.roll(x, ROPE_HALF, 1))
    return x * cos + partner * sin_signed


def _lam(lam_ref, lam_init):
    lp = lam_ref[...]
    a = jnp.sum(lp[0:1] * lp[1:2], axis=-1, keepdims=True)
    b = jnp.sum(lp[2:3] * lp[3:4], axis=-1, keepdims=True)
    return jnp.exp(a) - jnp.exp(b) + lam_init


def _subln(o, g, lam_init):
    o = o * lax.rsqrt(jnp.mean(o * o, axis=-1, keepdims=True) + EPS)
    return o * g * (1.0 - lam_init)


def _memkv_kernel(x_ref, g_ref, w_ref, k_ref, v_ref):
    hn = _rms(x_ref[0], g_ref[0]).astype(BF16)
    kv_t = _dot(hn, w_ref[0]).T
    k_ref[0, 0] = kv_t[:MEM_W]
    v_ref[0, 0] = kv_t[MEM_W:]


def _memkv(mem, g_mem, w_bf):
    depth, d, n = w_bf.shape
    b, n_mem, _ = mem.shape
    out = jax.ShapeDtypeStruct((depth, b, MEM_W, n_mem), F32)
    return pl.pallas_call(
        _memkv_kernel,
        grid=(depth, b),
        in_specs=[
            pl.BlockSpec((1, n_mem, d), lambda l, bi: (bi, 0, 0)),
            pl.BlockSpec((1, 1, d), lambda l, bi: (l, 0, 0)),
            pl.BlockSpec((1, d, n), lambda l, bi: (l, 0, 0)),
        ],
        out_specs=[pl.BlockSpec((1, 1, MEM_W, n_mem), lambda l, bi: (l, bi, 0, 0))] * 2,
        out_shape=[out, out],
        compiler_params=_cparams("arbitrary", "arbitrary"),
        name="memkv",
    )(mem, g_mem.reshape(depth, 1, d), w_bf)


def _mixa_kernel(x_ref, g_ref, win_ref, cw_ref, wout_ref, mk_ref, mv_ref, prev_ref, h_ref, st_ref, ubuf,
                 *, tm, c, n_t):
    i = pl.program_id(1)
    x = x_ref[...]
    hn = _rms(x, g_ref[...]).astype(BF16)
    proj = _dot(hn, win_ref[...])
    gate_b = proj[:, :c]
    u = proj[:, c:2 * c] * proj[:, 2 * c:3 * c]
    qm = proj[:, 3 * c:]

    @pl.when(i == 0)
    def _():
        ubuf[SUBLANES - 2:SUBLANES, :] = prev_ref[0]

    @pl.when(i > 0)
    def _():
        ubuf[SUBLANES - 2:SUBLANES, :] = ubuf[tm + SUBLANES - 2:tm + SUBLANES, :]

    ubuf[SUBLANES:SUBLANES + tm, :] = u
    cw = cw_ref[...]
    y = cw[0:1] * ubuf[SUBLANES - 2:SUBLANES - 2 + tm, :]
    y = y + cw[1:2] * ubuf[SUBLANES - 1:SUBLANES - 1 + tm, :]
    y = y + cw[2:3] * u
    mix = gate_b * y
    mo = _mem_attn(qm, mk_ref[0], mv_ref[0])
    out = _dot(mix.astype(BF16), wout_ref[:c, :]) + _dot(mo.astype(BF16), wout_ref[c:, :])
    h_ref[...] = x + out

    @pl.when(i == n_t - 1)
    def _():
        st_ref[0] = ubuf[tm + SUBLANES - 2:tm + SUBLANES, :]


def _mem_spec(n_mem):
    return pl.BlockSpec((1, MEM_W, n_mem), lambda bi, i: (bi, 0, 0))


def _mixa(x2d, b, t, tm, g, win_bf, conv_w, wout_bf, mem_k, mem_v, prev):
    d = x2d.shape[1]
    c = d - MEM_W
    n_t = t // tm
    n_mem = mem_k.shape[2]
    kern = functools.partial(_mixa_kernel, tm=tm, c=c, n_t=n_t)
    return pl.pallas_call(
        kern,
        grid=(b, n_t),
        in_specs=[
            pl.BlockSpec((tm, d), lambda bi, i: (bi * n_t + i, 0)),
            _resident((1, d)),
            _resident(win_bf.shape),
            _resident(conv_w.shape),
            _resident(wout_bf.shape),
            _mem_spec(n_mem),
            _mem_spec(n_mem),
            pl.BlockSpec((1, CONV_WIDTH - 1, c), lambda bi, i: (bi, 0, 0)),
        ],
        out_specs=[
            pl.BlockSpec((tm, d), lambda bi, i: (bi * n_t + i, 0)),
            pl.BlockSpec((1, CONV_WIDTH - 1, c), lambda bi, i: (bi, 0, 0)),
        ],
        out_shape=[
            jax.ShapeDtypeStruct(x2d.shape, F32),
            jax.ShapeDtypeStruct((b, CONV_WIDTH - 1, c), F32),
        ],
        scratch_shapes=[pltpu.VMEM((tm + SUBLANES, c), F32)],
        compiler_params=_cparams("arbitrary", "arbitrary"),
        name="mixa",
    )(x2d, g.reshape(1, d), win_bf, conv_w, wout_bf, mem_k, mem_v, prev)


def _mixa_short_kernel(x_ref, g_ref, win_ref, cw_ref, wout_ref, mk_ref, mv_ref, prev_ref, h_ref, st_ref,
                       *, c, n_seq, t):
    x = x_ref[...]
    hn = _rms(x, g_ref[...]).astype(BF16)
    proj = _dot(hn, win_ref[...])
    cw = cw_ref[...]
    row = lax.broadcasted_iota(jnp.int32, (t, c), 0)
    mixed = []
    for s in range(n_seq):
        p = proj[s * t:(s + 1) * t]
        u = p[:, c:2 * c] * p[:, 2 * c:3 * c]
        prev = prev_ref[s]
        back1 = jnp.where(row == 0, prev[1:2], pltpu.roll(u, 1, 0))
        back2 = jnp.where(row == 0, prev[0:1], jnp.where(row == 1, prev[1:2], pltpu.roll(u, 2, 0)))
        y = cw[0:1] * back2 + cw[1:2] * back1 + cw[2:3] * u
        mo = _mem_attn(p[:, 3 * c:], mk_ref[s], mv_ref[s])
        mixed.append(jnp.concatenate([p[:, :c] * y, mo], axis=1))
        st_ref[s] = u[t - (CONV_WIDTH - 1):]
    h_ref[...] = x + _dot(jnp.concatenate(mixed, axis=0).astype(BF16), wout_ref[...])


def _mixa_short(x2d, b, t, n_seq, g, win_bf, conv_w, wout_bf, mem_k, mem_v, prev):
    d = x2d.shape[1]
    c = d - MEM_W
    n_mem = mem_k.shape[2]
    kern = functools.partial(_mixa_short_kernel, c=c, n_seq=n_seq, t=t)
    seqs = lambda i: (i, 0, 0)
    return pl.pallas_call(
        kern,
        grid=(b // n_seq,),
        in_specs=[
            pl.BlockSpec((n_seq * t, d), lambda i: (i, 0)),
            _resident((1, d)),
            _resident(win_bf.shape),
            _resident(conv_w.shape),
            _resident(wout_bf.shape),
            pl.BlockSpec((n_seq, MEM_W, n_mem), seqs),
            pl.BlockSpec((n_seq, MEM_W, n_mem), seqs),
            pl.BlockSpec((n_seq, CONV_WIDTH - 1, c), seqs),
        ],
        out_specs=[pl.BlockSpec((n_seq * t, d), lambda i: (i, 0)), pl.BlockSpec((n_seq, CONV_WIDTH - 1, c), seqs)],
        out_shape=[jax.ShapeDtypeStruct(x2d.shape, F32), jax.ShapeDtypeStruct((b, CONV_WIDTH - 1, c), F32)],
        compiler_params=_cparams("arbitrary"),
        name="mixa_short",
    )(x2d, g.reshape(1, d), win_bf, conv_w, wout_bf, mem_k, mem_v, prev)


def _silu(g):
    return g / (1.0 + jnp.exp(-g))


def _ffn_kernel(h_ref, g_ref, wg_ref, wu_ref, wd_ref, o_ref):
    h = h_ref[...]
    hn = _rms(h, g_ref[...]).astype(BF16)
    a = _silu(_dot(hn, wg_ref[...])) * _dot(hn, wu_ref[...])
    o_ref[...] = h + _dot(a.astype(BF16), wd_ref[...])


def _ffn(h2d, tm, g, wg_bf, wu_bf, wd_bf):
    m, d = h2d.shape
    return pl.pallas_call(
        _ffn_kernel,
        grid=(m // tm,),
        in_specs=[
            pl.BlockSpec((tm, d), lambda i: (i, 0)),
            _resident((1, d)),
            _resident(wg_bf.shape),
            _resident(wu_bf.shape),
            _resident(wd_bf.shape),
        ],
        out_specs=pl.BlockSpec((tm, d), lambda i: (i, 0)),
        out_shape=jax.ShapeDtypeStruct(h2d.shape, F32),
        compiler_params=_cparams("arbitrary"),
        name="ffn",
    )(h2d, g.reshape(1, d), wg_bf, wu_bf, wd_bf)


def _kvq_kernel(h_ref, gkv_ref, gq_ref, wkv_ref, wq_ref, cos_ref, sin_ref,
                k_ref, v_ref, kb_ref, vb_ref, q_ref, qm_ref, *, n_heads):
    x = h_ref[...]
    xn = x * lax.rsqrt(jnp.mean(x * x, axis=-1, keepdims=True) + EPS)
    cos = cos_ref[...]
    sin = sin_ref[...]
    kv = _dot((xn * gkv_ref[...]).astype(BF16), wkv_ref[...])
    for h in range(n_heads):
        r = _rope_slab(kv[:, h * HEAD_W:(h + 1) * HEAD_W], cos, sin)
        k_ref[0, h] = r
        kb_ref[0, h] = r.astype(BF16)
        v = kv[:, (n_heads + h) * HEAD_W:(n_heads + h + 1) * HEAD_W]
        v_ref[0, h] = v
        vb_ref[0, h] = v.astype(BF16)
    p = _dot((xn * gq_ref[...]).astype(BF16), wq_ref[...])
    for h in range(n_heads):
        q_ref[0, h] = (_rope_slab(p[:, h * HEAD_W:(h + 1) * HEAD_W], cos, sin) * QK_SCALE).astype(BF16)
    qm_ref[...] = p[:, n_heads * HEAD_W:]


def _rope_tables(pos):
    inv = ROPE_THETA ** (-jnp.arange(ROPE_HALF, dtype=F32) / ROPE_HALF)
    ang = pos.astype(F32)[:, None] * inv[None, :]
    cos = jnp.tile(jnp.cos(ang), (1, HEAD_W // ROPE_HALF))
    sin = jnp.sin(ang)
    sin = jnp.tile(jnp.concatenate([-sin, sin], axis=-1), (1, HEAD_W // HALF_W))
    return cos, sin


def _kvqproj(h2d, b, t, tm, g_kv, wkv_bf, g_q, wq_bf, cos, sin):
    m, d = h2d.shape
    n_heads = wkv_bf.shape[1] // (2 * HEAD_W)
    n_t = t // tm
    kern = functools.partial(_kvq_kernel, n_heads=n_heads)
    head_major = pl.BlockSpec((1, n_heads, tm, HEAD_W), lambda bi, i: (bi, 0, i, 0))
    rope = pl.BlockSpec((tm, HEAD_W), lambda bi, i: (i, 0))
    rows = lambda bi, i: (bi * n_t + i, 0)
    hm_f32 = jax.ShapeDtypeStruct((b, n_heads, t, HEAD_W), F32)
    hm_bf16 = jax.ShapeDtypeStruct((b, n_heads, t, HEAD_W), BF16)
    return pl.pallas_call(
        kern,
        grid=(b, n_t),
        in_specs=[pl.BlockSpec((tm, d), rows), _resident((1, d)), _resident((1, d)), _resident(wkv_bf.shape),
                  _resident(wq_bf.shape), rope, rope],
        out_specs=[head_major] * 5 + [pl.BlockSpec((tm, MEM_W), rows)],
        out_shape=[hm_f32, hm_f32, hm_bf16, hm_bf16, hm_bf16, jax.ShapeDtypeStruct((m, MEM_W), F32)],
        compiler_params=_cparams("arbitrary", "arbitrary"),
        name="kvqproj",
    )(h2d, g_kv.reshape(1, d), g_q.reshape(1, d), wkv_bf, wq_bf, cos, sin)


def _stack_components(q):
    lane = lax.broadcasted_iota(jnp.int32, q.shape, 1)
    zero = jnp.zeros_like(q)
    return jnp.concatenate([jnp.where(lane < HALF_W, q, zero), jnp.where(lane >= HALF_W, q, zero)], axis=0)


def _online_update(s, pv, m_ref, l_ref, acc_ref):
    cols = [s[:, c * LANES:(c + 1) * LANES] for c in range(s.shape[1] // LANES)]
    m_prev = m_ref[...]
    m_new = jnp.maximum(m_prev, jnp.max(functools.reduce(jnp.maximum, cols), axis=-1, keepdims=True))
    alpha = jnp.exp2(m_prev - m_new)
    ps = [jnp.exp2(c - m_new) for c in cols]
    l_ref[...] = alpha * l_ref[...] + functools.reduce(jnp.add, ps)
    acc_ref[...] = alpha * acc_ref[...] + pv(jnp.concatenate(ps, axis=1).astype(BF16))
    m_ref[...] = m_new


def _init_online(m_ref, l_ref, acc_ref):
    m_ref[...] = jnp.full(m_ref.shape, NEG, F32)
    l_ref[...] = jnp.zeros(l_ref.shape, F32)
    acc_ref[...] = jnp.zeros(acc_ref.shape, F32)


def _diff_combine(l_ref, acc_ref, t, lam):
    o = acc_ref[...] / jnp.sum(l_ref[...], axis=-1, keepdims=True)
    return o[:t] - lam * o[t:]


def _attn_prompt_kernel(q_ref, k_ref, v_ref, lam_ref, sg_ref, o_ref, m_ref, l_ref, acc_ref, *, tq, lam_init):
    qi = pl.program_id(2)
    qs = _stack_components(q_ref[0, 0])
    _init_online(m_ref, l_ref, acc_ref)

    def chunk(j, masked):
        start = pl.multiple_of(j * tq, tq)
        s = _dot_t(qs, k_ref[0, 0, pl.ds(start, tq), :])
        if masked:
            row = lax.broadcasted_iota(jnp.int32, s.shape, 0) % tq
            col = lax.broadcasted_iota(jnp.int32, s.shape, 1)
            s = jnp.where(col <= row, s, NEG)
        v = v_ref[0, 0, pl.ds(start, tq), :]
        _online_update(s, lambda p: _dot(p, v), m_ref, l_ref, acc_ref)

    def body(j, carry):
        chunk(2 * j, False)
        chunk(2 * j + 1, False)
        return carry

    lax.fori_loop(0, qi // 2, body, 0)

    @pl.when(qi % 2 == 1)
    def _():
        chunk(qi - 1, False)

    chunk(qi, True)
    o = _diff_combine(l_ref, acc_ref, tq, _lam(lam_ref, lam_init))
    o_ref[...] = _subln(o, sg_ref[...], lam_init).astype(BF16)


def _attn_prompt(q_bf, k_bf, v_bf, tq, lam_p, subln_g, lam_init):
    b, n_heads, t, _ = q_bf.shape
    nq = t // tq
    kern = functools.partial(_attn_prompt_kernel, tq=tq, lam_init=lam_init)
    seq = pl.BlockSpec((1, 1, t, HEAD_W), lambda bi, h, qi: (bi, h, 0, 0))
    return pl.pallas_call(
        kern,
        grid=(b, n_heads, nq),
        in_specs=[
            pl.BlockSpec((1, 1, tq, HEAD_W), lambda bi, h, qi: (bi, h, qi, 0)),
            seq,
            seq,
            pl.BlockSpec(lam_p.shape, lambda bi, h, qi: (0, 0)),
            pl.BlockSpec((1, HEAD_W), lambda bi, h, qi: (0, 0)),
        ],
        out_specs=pl.BlockSpec((tq, HEAD_W), lambda bi, h, qi: (bi * nq + qi, h)),
        out_shape=jax.ShapeDtypeStruct((b * t, n_heads * HEAD_W), BF16),
        scratch_shapes=[pltpu.VMEM((2 * tq, LANES), F32)] * 3,
        compiler_params=_cparams("arbitrary", "arbitrary", "arbitrary"),
        name="attn_prompt",
    )(q_bf, k_bf, v_bf, lam_p, subln_g.reshape(1, HEAD_W))


def _attn_sample_kernel(pt_ref, q_ref, kn_ref, vn_ref, *rest, ts, n_heads, pages, n_groups, lam_init):
    k_refs = rest[:pages]
    v_refs = rest[pages:2 * pages]
    lam_ref, sg_ref, o_ref, qs_ref, m_ref, l_ref, acc_ref = rest[2 * pages:]
    g = pl.program_id(1)

    r = 2 * ts

    @pl.when(g == 0)
    def _():
        for h in range(n_heads):
            qs_ref[h] = _stack_components(q_ref[0, h])
        _init_online(m_ref, l_ref, acc_ref)

    def all_heads(keys, values):
        s = jnp.concatenate([_dot_t(qs_ref[h], keys(h)) for h in range(n_heads)], axis=0)

        def pv(p):
            return jnp.concatenate([_dot(p[h * r:(h + 1) * r], values(h)) for h in range(n_heads)], axis=0)

        return s, pv

    def past(refs):
        return lambda h: jnp.concatenate([x[0, h].astype(BF16) for x in refs], axis=0)

    s, pv = all_heads(past(k_refs), past(v_refs))
    _online_update(s, pv, m_ref, l_ref, acc_ref)

    @pl.when(g == n_groups - 1)
    def _():
        pad = jnp.zeros((LANES - ts, HEAD_W), BF16)

        def new(ref):
            return lambda h: jnp.concatenate([ref[0, h].astype(BF16), pad], axis=0)

        s, pv = all_heads(new(kn_ref), new(vn_ref))
        row = lax.broadcasted_iota(jnp.int32, s.shape, 0) % ts
        col = lax.broadcasted_iota(jnp.int32, s.shape, 1)
        _online_update(jnp.where(col <= row, s, NEG), pv, m_ref, l_ref, acc_ref)
        lam = _lam(lam_ref, lam_init)
        for h in range(n_heads):
            rows = pl.ds(h * r, r)
            o = _diff_combine(l_ref.at[rows], acc_ref.at[rows], ts, lam)
            o_ref[:, h * HEAD_W:(h + 1) * HEAD_W] = _subln(o, sg_ref[...], lam_init).astype(BF16)


def _attn_sample(q_bf, k_new, v_new, cache_k, cache_v, page_table, pages, lam_p, subln_g, lam_init):
    bs, n_heads, ts, _ = q_bf.shape
    ps = cache_k.shape[2]
    n_groups = page_table.shape[1] // pages
    kern = functools.partial(_attn_sample_kernel, ts=ts, n_heads=n_heads, pages=pages, n_groups=n_groups,
                             lam_init=lam_init)

    def page_spec(i):
        return pl.BlockSpec((1, n_heads, ps, HEAD_W), lambda bi, g, pt: (pt[bi, g * pages + i], 0, 0, 0))

    new_rows = pl.BlockSpec((1, n_heads, ts, HEAD_W), lambda bi, g, pt: (bi, 0, 0, 0))
    grid_spec = pltpu.PrefetchScalarGridSpec(
        num_scalar_prefetch=1,
        grid=(bs, n_groups),
        in_specs=[new_rows] * 3
        + [page_spec(i) for i in range(pages)] * 2
        + [pl.BlockSpec(lam_p.shape, lambda bi, g, pt: (0, 0)), pl.BlockSpec((1, HEAD_W), lambda bi, g, pt: (0, 0))],
        out_specs=pl.BlockSpec((ts, n_heads * HEAD_W), lambda bi, g, pt: (bi, 0)),
        scratch_shapes=[pltpu.VMEM((n_heads, 2 * ts, HEAD_W), BF16)]
        + [pltpu.VMEM((n_heads * 2 * ts, LANES), F32)] * 3,
    )
    return pl.pallas_call(
        kern,
        grid_spec=grid_spec,
        out_shape=jax.ShapeDtypeStruct((bs * ts, n_heads * HEAD_W), BF16),
        compiler_params=_cparams("arbitrary", "arbitrary"),
        name="attn_sample",
    )(page_table, q_bf, k_new, v_new, *([cache_k] * pages), *([cache_v] * pages), lam_p, subln_g.reshape(1, HEAD_W))


def _route(h, g, r_hi, r_lo, base, n_experts):
    hn = _rms(h, g)
    hi = hn.astype(BF16)
    lo = (hn - hi.astype(F32)).astype(BF16)
    logits = _dot(hi, r_hi) + (_dot(hi, r_lo) + _dot(lo, r_hi))
    lane = lax.broadcasted_iota(jnp.int32, logits.shape, 1)
    logits = jnp.where(lane < n_experts, logits, -jnp.inf)
    v1 = jnp.max(logits, axis=-1, keepdims=True)
    e1 = jnp.min(jnp.where(logits == v1, lane, LANES), axis=-1, keepdims=True)
    rest = jnp.where(lane == e1, -jnp.inf, logits)
    v2 = jnp.max(rest, axis=-1, keepdims=True)
    e2 = jnp.min(jnp.where(rest == v2, lane, LANES), axis=-1, keepdims=True)
    ex = jnp.exp(v2 - v1)
    g1 = 1.0 / (1.0 + ex)
    g2 = ex / (1.0 + ex)
    sel = jnp.where((lane == e1) | (lane == e2), 1.0, 0.0)
    t = h.shape[0]
    earlier = lax.broadcasted_iota(jnp.int32, (t, t), 0) > lax.broadcasted_iota(jnp.int32, (t, t), 1)
    cum = _dot(jnp.where(earlier, 1.0, 0.0).astype(BF16), sel.astype(BF16)) + base
    r1 = jnp.sum(jnp.where(lane == e1, cum, 0.0), axis=-1, keepdims=True)
    r2 = jnp.sum(jnp.where(lane == e2, cum, 0.0), axis=-1, keepdims=True)
    out = jnp.where(lane == 0, e1.astype(F32), 0.0)
    for i, val in enumerate((e2.astype(F32), g1, g2, r1, r2), start=1):
        out = jnp.where(lane == i, val, out)
    return out, sel


def _mixb_kernel(mix_ref, qm_ref, x_ref, wout_ref, mk_ref, mv_ref, gf_ref, rhi_ref, rlo_ref,
                 h_ref, route_ref, cnt_ref, carry, *, n_experts, n_seq):
    @pl.when((pl.program_id(0) == 0) & (pl.program_id(1) == 0))
    def _():
        carry[...] = jnp.zeros(carry.shape, F32)

    c = mix_ref.shape[1]
    qm = qm_ref[...]
    t = qm.shape[0] // n_seq
    mo = jnp.concatenate([_mem_attn(qm[s * t:(s + 1) * t], mk_ref[s], mv_ref[s]) for s in range(n_seq)], axis=0)
    h = x_ref[...] + (_dot(mix_ref[...], wout_ref[:c, :]) + _dot(mo.astype(BF16), wout_ref[c:, :]))
    h_ref[...] = h
    route, sel = _route(h, gf_ref[...], rhi_ref[...], rlo_ref[...], carry[...], n_experts)
    route_ref[...] = route
    carry[...] += jnp.sum(sel, axis=0, keepdims=True)
    cnt_ref[...] = carry[...]


def _mixb(mix_bf, qm, x2d, b, t, tm, wout_bf, mem_k, mem_v, g_ffn, router, n_seq=1):
    m, d = x2d.shape
    c = mix_bf.shape[1]
    assert n_seq == 1 or tm == t
    n_t = t // tm
    tm = tm * n_seq
    n_mem = mem_k.shape[2]
    n_experts = router.shape[1]
    r_pad = jnp.zeros((d, LANES), F32).at[:, :n_experts].set(router)
    r_hi = r_pad.astype(BF16)
    r_lo = (r_pad - r_hi.astype(F32)).astype(BF16)
    row = lambda bi, i: (bi * n_t + i, 0)
    mem = pl.BlockSpec((n_seq, MEM_W, n_mem), lambda bi, i: (bi, 0, 0))
    kern = functools.partial(_mixb_kernel, n_experts=n_experts, n_seq=n_seq)
    return pl.pallas_call(
        kern,
        grid=(b // n_seq, n_t),
        in_specs=[
            pl.BlockSpec((tm, c), row),
            pl.BlockSpec((tm, MEM_W), row),
            pl.BlockSpec((tm, d), row),
            _resident(wout_bf.shape),
            mem,
            mem,
            _resident((1, d)),
            _resident((d, LANES)),
            _resident((d, LANES)),
        ],
        out_specs=[
            pl.BlockSpec((tm, d), row),
            pl.BlockSpec((tm, LANES), row),
            pl.BlockSpec((1, LANES), lambda bi, i: (0, 0)),
        ],
        out_shape=[
            jax.ShapeDtypeStruct((m, d), F32),
            jax.ShapeDtypeStruct((m, LANES), F32),
            jax.ShapeDtypeStruct((1, LANES), F32),
        ],
        scratch_shapes=[pltpu.VMEM((1, LANES), F32)],
        compiler_params=_cparams("arbitrary", "arbitrary"),
        name="mixb",
    )(mix_bf, qm, x2d, wout_bf, mem_k, mem_v, g_ffn.reshape(1, d), r_hi, r_lo)


GATHER_UNROLL = 8
SEQS_PER_STEP = 8


def _row_copy(src_hbm, row, dst, slot, sem):
    return pltpu.make_async_copy(src_hbm.at[pl.ds(row, 1)], dst.at[pl.ds(slot, 1)], sem)


def _gather_start(src_hbm, idx_ref, base, dst, sem, n):
    def issue(j, carry):
        for u in range(GATHER_UNROLL):
            r = j * GATHER_UNROLL + u
            _row_copy(src_hbm, idx_ref[base + r], dst, r, sem).start(priority=u % 2)
        return carry

    lax.fori_loop(0, n // GATHER_UNROLL, issue, 0)


def _gather_start_inline(src_hbm, idx_ref, base, dst, sem, n):
    for r in range(n):
        _row_copy(src_hbm, idx_ref[base + r], dst, r, sem).start(priority=r % 2)


def _gather_wait(src_hbm, dst, sem, n):
    def wait(r, carry):
        _row_copy(src_hbm, 0, dst, r, sem).wait()
        return carry

    lax.fori_loop(0, n, wait, 0, unroll=GATHER_UNROLL)


def _expert_kernel(blk_e_ref, tok_ref, nused_ref, h_hbm, g_ref, wg_ref, wu_ref, wd_ref, o_ref, xbuf, sem, *, blk):
    i = pl.program_id(0)
    n_used = nused_ref[0]
    slot = i % 2

    @pl.when(i == 0)
    def _():
        _gather_start(h_hbm, tok_ref, 0, xbuf.at[0], sem.at[0], blk)

    @pl.when(i + 1 < n_used)
    def _():
        _gather_start(h_hbm, tok_ref, (i + 1) * blk, xbuf.at[1 - slot], sem.at[1 - slot], blk)

    @pl.when(i < n_used)
    def _():
        _gather_wait(h_hbm, xbuf.at[slot], sem.at[slot], blk)
        hn = _rms(xbuf[slot], g_ref[...]).astype(BF16)
        a = _silu(_dot(hn, wg_ref[0])) * _dot(hn, wu_ref[0])
        o_ref[...] = _dot(a.astype(BF16), wd_ref[0])

    @pl.when(i >= n_used)
    def _():
        o_ref[...] = jnp.zeros(o_ref.shape, F32)


def _experts(h2d, g, wg_bf, wu_bf, wd_bf, blk_e, buf_tok, n_used, blk):
    n_rows = buf_tok.shape[0]
    n_blk = n_rows // blk
    _, d, ff = wg_bf.shape
    kern = functools.partial(_expert_kernel, blk=blk)
    grid_spec = pltpu.PrefetchScalarGridSpec(
        num_scalar_prefetch=3,
        grid=(n_blk,),
        in_specs=[
            pl.BlockSpec(memory_space=pl.ANY),
            pl.BlockSpec((1, d), lambda i, be, tk, nu: (0, 0)),
            pl.BlockSpec((1, d, ff), lambda i, be, tk, nu: (be[i], 0, 0)),
            pl.BlockSpec((1, d, ff), lambda i, be, tk, nu: (be[i], 0, 0)),
            pl.BlockSpec((1, ff, d), lambda i, be, tk, nu: (be[i], 0, 0)),
        ],
        out_specs=pl.BlockSpec((blk, d), lambda i, be, tk, nu: (i, 0)),
        scratch_shapes=[pltpu.VMEM((2, blk, d), F32), pltpu.SemaphoreType.DMA((2,))],
    )
    return pl.pallas_call(
        kern,
        grid_spec=grid_spec,
        out_shape=jax.ShapeDtypeStruct((n_rows, d), F32),
        compiler_params=_cparams("arbitrary"),
        name="experts",
    )(blk_e, buf_tok, n_used, h2d, g.reshape(1, d), wg_bf, wu_bf, wd_bf)


def _combine_kernel(pos_ref, h_ref, route_ref, g_ref, yb_hbm, o_ref, ybuf, sem, *, tc, n_steps):
    i = pl.program_id(0)
    slot = i % 2

    @pl.when(i == 0)
    def _():
        for k in range(TOP_K):
            _gather_start(yb_hbm, pos_ref, k * n_steps * tc, ybuf.at[0, k], sem.at[0], tc)

    for k in range(TOP_K):
        _gather_wait(yb_hbm, ybuf.at[slot, k], sem.at[slot], tc)
    nxt = jnp.minimum(i + 1, n_steps - 1)
    for k in range(TOP_K):
        _gather_start_inline(yb_hbm, pos_ref, (k * n_steps + nxt) * tc, ybuf.at[1 - slot, k], sem.at[1 - slot], tc)
    route = route_ref[...]
    y = ybuf[slot, 0] * route[:, 2:3] + ybuf[slot, 1] * route[:, 3:4]
    o_ref[...] = _rms(h_ref[...] + y, g_ref[...])

    @pl.when(i == n_steps - 1)
    def _():
        for k in range(TOP_K):
            _gather_wait(yb_hbm, ybuf.at[1 - slot, k], sem.at[1 - slot], tc)


def _combine(h2d, route, yb, pos_k_major, g, tc):
    m, d = h2d.shape
    n_steps = m // tc
    kern = functools.partial(_combine_kernel, tc=tc, n_steps=n_steps)
    grid_spec = pltpu.PrefetchScalarGridSpec(
        num_scalar_prefetch=1,
        grid=(n_steps,),
        in_specs=[
            pl.BlockSpec((tc, d), lambda i, p: (i, 0)),
            pl.BlockSpec((tc, LANES), lambda i, p: (i, 0)),
            pl.BlockSpec((1, d), lambda i, p: (0, 0)),
            pl.BlockSpec(memory_space=pl.ANY),
        ],
        out_specs=pl.BlockSpec((tc, d), lambda i, p: (i, 0)),
        scratch_shapes=[pltpu.VMEM((2, TOP_K, tc, d), F32), pltpu.SemaphoreType.DMA((2,))],
    )
    return pl.pallas_call(
        kern,
        grid_spec=grid_spec,
        out_shape=jax.ShapeDtypeStruct((m, d), F32),
        compiler_params=_cparams("arbitrary"),
        name="combine",
    )(pos_k_major, h2d, route, g.reshape(1, d), yb)


def _slot_tokens_kernel(dest_ref, o_ref, *, n_assign, n_rows):
    def zero(p, carry):
        o_ref[p] = 0
        return carry

    lax.fori_loop(0, n_rows, zero, 0, unroll=GATHER_UNROLL)

    def put(j, carry):
        for u in range(GATHER_UNROLL):
            tok = j * GATHER_UNROLL + u
            for k in range(TOP_K):
                o_ref[dest_ref[tok * TOP_K + k]] = tok
        return carry

    lax.fori_loop(0, n_assign // (TOP_K * GATHER_UNROLL), put, 0)


def _slot_tokens(dest_flat, n_rows):
    n_assign = dest_flat.shape[0]
    kern = functools.partial(_slot_tokens_kernel, n_assign=n_assign, n_rows=n_rows)
    grid_spec = pltpu.PrefetchScalarGridSpec(
        num_scalar_prefetch=1, grid=(1,), in_specs=[], out_specs=pl.BlockSpec(memory_space=pltpu.SMEM))
    return pl.pallas_call(
        kern,
        grid_spec=grid_spec,
        out_shape=jax.ShapeDtypeStruct((n_rows,), jnp.int32),
        compiler_params=_cparams("arbitrary"),
        name="slot_tokens",
    )(dest_flat)


def _moe(h2d, route, counts, tc, g_ffn, wg_bf, wu_bf, wd_bf, g_final, blk):
    m, d = h2d.shape
    n_experts = wg_bf.shape[0]
    a = m * TOP_K
    counts = counts[0, :n_experts].astype(jnp.int32)
    padded = (counts + blk - 1) // blk * blk
    cum_pad = jnp.cumsum(padded)
    top_e = route[:, :TOP_K].astype(jnp.int32)
    dest = (cum_pad - padded)[top_e] + route[:, 2 * TOP_K:3 * TOP_K].astype(jnp.int32)
    n_blk = -(-(a + n_experts * (blk - 1)) // blk)
    buf_tok = _slot_tokens(dest.reshape(-1), n_blk * blk)
    n_used = (cum_pad[-1] // blk).astype(jnp.int32)
    blk_i = jnp.minimum(jnp.arange(n_blk, dtype=jnp.int32), n_used - 1)
    blk_e = jnp.sum((blk_i[:, None] * blk >= cum_pad[None, :]).astype(jnp.int32), axis=1)
    blk_e = jnp.minimum(blk_e, n_experts - 1)
    yb = _experts(h2d, g_ffn, wg_bf, wu_bf, wd_bf, blk_e, buf_tok, n_used.reshape(1), blk)
    return _combine(h2d, route, yb, dest.T.reshape(-1), g_final, tc)


def _head_major(x, b, t):
    n_heads = x.shape[1]
    return jnp.transpose(x.reshape(n_heads, b, t, HEAD_W), (1, 0, 2, 3))


def _trunk(x, pos, conv_prev, mem_k, mem_v, attend, w, tm_a, tm, tok_tm, moe_blk):
    b, t, d = x.shape
    x2d = x.reshape(b * t, d)
    short = t == SUBLANES
    n_seq = math.gcd(b, SEQS_PER_STEP) if short else 1
    mix_a = (w['g_mix'][0], w['a_w_in'], w['a_conv'], w['a_w_out'], mem_k[0], mem_v[0], conv_prev)
    h, conv_st = _mixa_short(x2d, b, t, n_seq, *mix_a) if short else _mixa(x2d, b, t, tm_a, *mix_a)
    h = _ffn(h, tok_tm, w['g_ffn'][0], w['f_w_gate'], w['f_w_up'], w['f_w_down'])
    cos, sin = _rope_tables(pos)
    proj = (w['g_kv'], w['w_kv'], w['g_mix'][1], w['b_w_in'])
    if tok_tm > t:
        cos, sin = jnp.tile(cos, (b, 1)), jnp.tile(sin, (b, 1))
        *heads, qm = _kvqproj(h, 1, b * t, tok_tm, *proj, cos, sin)
        k, v, k_bf, v_bf, q_bf = (_head_major(o, b, t) for o in heads)
    else:
        k, v, k_bf, v_bf, q_bf, qm = _kvqproj(h, b, t, tok_tm, *proj, cos, sin)
    mix = attend(q_bf, k, v, k_bf, v_bf)
    h, route, counts = _mixb(mix, qm, h, b, t, tm, w['b_w_out'], mem_k[1], mem_v[1], w['g_ffn'][1], w['m_router'],
                             n_seq)
    y = _moe(h, route, counts, tok_tm, w['g_ffn'][1], w['m_w_gate'], w['m_w_up'], w['m_w_down'], w['g_final'],
             moe_blk)
    return y.reshape(b, t, d), conv_st, k, v


def _token_major(x):
    return jnp.transpose(x, (0, 2, 1, 3))


def kernel(x_prompt, x_sample, state_conv, cache_k, cache_v, cache_mem_k, cache_mem_v, page_table, mem_prompt, g_mix, g_ffn, g_mem, w_mem_kv, a_w_in, a_conv, a_w_out, g_kv, w_kv, b_w_in, b_lambda, b_subln, b_w_out, f_w_gate, f_w_up, f_w_down, m_router, m_w_gate, m_w_up, m_w_down, g_final):
    bp, tp, d = x_prompt.shape
    bs, ts, _ = x_sample.shape
    depth = g_mix.shape[0]
    n_a = a_w_in.shape[0]
    assert depth == 2 and n_a == 1 and b_w_in.shape[0] == 1, "one conv layer followed by one attention layer"
    assert ts == SUBLANES and tp % 512 == 0
    c = d - MEM_W
    lam_init = 0.8 - 0.6 * math.exp(-0.3 * n_a)
    w = {
        'g_mix': g_mix, 'g_ffn': g_ffn, 'g_kv': g_kv, 'g_final': g_final,
        'a_w_in': a_w_in[0].astype(BF16), 'a_conv': a_conv[0], 'a_w_out': a_w_out[0].astype(BF16),
        'w_kv': w_kv.astype(BF16), 'b_w_in': b_w_in[0].astype(BF16), 'b_w_out': b_w_out[0].astype(BF16),
        'f_w_gate': f_w_gate[0].astype(BF16), 'f_w_up': f_w_up[0].astype(BF16), 'f_w_down': f_w_down[0].astype(BF16),
        'm_router': m_router[0], 'm_w_gate': m_w_gate[0].astype(BF16), 'm_w_up': m_w_up[0].astype(BF16),
        'm_w_down': m_w_down[0].astype(BF16),
    }
    lam_p = b_lambda[0]
    subln_g = b_subln[0]

    n_mem = mem_prompt.shape[1]
    mem_kt, mem_vt = _memkv(mem_prompt, g_mem, w_mem_kv.astype(BF16))

    def mem_out(x):
        return jnp.transpose(x.reshape(depth, bp, MEM_HEADS, MEM_HEAD_DIM, n_mem), (0, 1, 4, 2, 3))

    def attend_prompt(q_bf, k, v, k_bf, v_bf):
        return _attn_prompt(q_bf, k_bf, v_bf, 512, lam_p, subln_g, lam_init)

    y_p, conv_p, k_p, v_p = _trunk(
        x_prompt, jnp.arange(tp), jnp.zeros((bp, CONV_WIDTH - 1, c), F32), mem_kt, mem_vt, attend_prompt, w,
        tm_a=512, tm=256, tok_tm=256, moe_blk=256)

    past = page_table.shape[1] * cache_k.shape[1]
    pages = math.gcd(page_table.shape[1], 16)
    cache_kh = jnp.transpose(cache_k, (0, 2, 1, 3))
    cache_vh = jnp.transpose(cache_v, (0, 2, 1, 3))

    def mem_in(x):
        return jnp.transpose(x, (0, 1, 3, 4, 2)).reshape(depth, bs, MEM_W, x.shape[2])

    def attend_sample(q_bf, k, v, k_bf, v_bf):
        return _attn_sample(q_bf, k, v, cache_kh, cache_vh, page_table, pages, lam_p, subln_g, lam_init)

    y_s, conv_s, k_s, v_s = _trunk(
        x_sample, past + jnp.arange(ts), state_conv[0], mem_in(cache_mem_k), mem_in(cache_mem_v), attend_sample, w,
        tm_a=ts, tm=ts, tok_tm=bs * ts, moe_blk=128)

    return (y_p, y_s, conv_p[None], conv_s[None], _token_major(k_p), _token_major(v_p), _token_major(k_s),
            _token_major(v_s), mem_out(mem_kt), mem_out(mem_vt))
```

```python
import functools
import math

import jax
import jax.numpy as jnp
from jax import lax
from jax.experimental import pallas as pl
from jax.experimental.pallas import tpu as pltpu

F32 = jnp.float32
BF16 = jnp.bfloat16

EPS = 1e-6
MEM_HEADS = 4
MEM_HEAD_DIM = 64
MEM_W = MEM_HEADS * MEM_HEAD_DIM
HEAD_W = 128
HALF_W = HEAD_W // 2
ROPE_HALF = HALF_W // 2
ROPE_THETA = 10000.0
CONV_WIDTH = 3
TOP_K = 2
NEG = -1e30
QK_SCALE = HALF_W ** -0.5 * math.log2(math.e)
MEM_SCALE = MEM_HEAD_DIM ** -0.5
LANES = 128
SUBLANES = 8
VMEM_LIMIT = 56 * 1024 * 1024


def _cparams(*sem):
    return pltpu.CompilerParams(dimension_semantics=sem, vmem_limit_bytes=VMEM_LIMIT)


def _rms(x, g):
    return x * lax.rsqrt(jnp.mean(x * x, axis=-1, keepdims=True) + EPS) * g


def _dot(a, b):
    return jnp.dot(a, b, preferred_element_type=F32)


def _dot_t(a, b):
    return lax.dot_general(a, b, (((1,), (1,)), ((), ())), preferred_element_type=F32)


def _resident(shape):
    n = len(shape)
    return pl.BlockSpec(shape, lambda *_: (0,) * n, pipeline_mode=pl.Buffered(1))


def _mem_attn(qm, mk_t, mv_t):
    q = (qm * MEM_SCALE).astype(BF16)
    kb = mk_t.astype(BF16)
    vb = mv_t.astype(BF16)
    t = q.shape[0]
    lane = lax.broadcasted_iota(jnp.int32, q.shape, 1)
    in_head = [(lane >= h * MEM_HEAD_DIM) & (lane < (h + 1) * MEM_HEAD_DIM) for h in range(MEM_HEADS)]
    s = _dot(jnp.concatenate([jnp.where(sel, q, jnp.zeros_like(q)) for sel in in_head], axis=0), kb)
    m = jnp.max(s, axis=-1, keepdims=True)
    p = jnp.exp(s - m)
    p = p / jnp.sum(p, axis=-1, keepdims=True)
    pv = _dot_t(p.astype(BF16), vb)
    out = jnp.zeros(q.shape, F32)
    for h, sel in enumerate(in_head):
        out = jnp.where(sel, pv[h * t:(h + 1) * t], out)
    return out


def _rope_slab(x, cos, sin_signed):
    lane = lax.broadcasted_iota(jnp.int32, x.shape, 1)
    first_half = (lane % HALF_W) < ROPE_HALF
    partner = jnp.where(first_half, pltpu.roll(x, HEAD_W - ROPE_HALF, 1), pltpu.roll(x, ROPE_HALF, 1))
    return x * cos + partner * sin_signed


def _lam(lam_ref, lam_init):
    lp = lam_ref[...]
    a = jnp.sum(lp[0:1] * lp[1:2], axis=-1, keepdims=True)
    b = jnp.sum(lp[2:3] * lp[3:4], axis=-1, keepdims=True)
    return jnp.exp(a) - jnp.exp(b) + lam_init


def _subln(o, g, lam_init):
    o = o * lax.rsqrt(jnp.mean(o * o, axis=-1, keepdims=True) + EPS)
    return o * g * (1.0 - lam_init)


def _memkv_kernel(x_ref, g_ref, w_ref, k_ref, v_ref):
    hn = _rms(x_ref[0], g_ref[0]).astype(BF16)
    kv_t = _dot(hn, w_ref[0]).T
    k_ref[0, 0] = kv_t[:MEM_W]
    v_ref[0, 0] = kv_t[MEM_W:]


def _memkv(mem, g_mem, w_bf):
    depth, d, n = w_bf.shape
    b, n_mem, _ = mem.shape
    out = jax.ShapeDtypeStruct((depth, b, MEM_W, n_mem), F32)
    return pl.pallas_call(
        _memkv_kernel,
        grid=(depth, b),
        in_specs=[
            pl.BlockSpec((1, n_mem, d), lambda l, bi: (bi, 0, 0)),
            pl.BlockSpec((1, 1, d), lambda l, bi: (l, 0, 0)),
            pl.BlockSpec((1, d, n), lambda l, bi: (l, 0, 0)),
        ],
        out_specs=[pl.BlockSpec((1, 1, MEM_W, n_mem), lambda l, bi: (l, bi, 0, 0))] * 2,
        out_shape=[out, out],
        compiler_params=_cparams("arbitrary", "arbitrary"),
        name="memkv",
    )(mem, g_mem.reshape(depth, 1, d), w_bf)


def _mixa_kernel(x_ref, g_ref, win_ref, cw_ref, wout_ref, mk_ref, mv_ref, prev_ref, h_ref, st_ref, ubuf,
                 *, tm, c, n_t):
    i = pl.program_id(1)
    x = x_ref[...]
    hn = _rms(x, g_ref[...]).astype(BF16)
    proj = _dot(hn, win_ref[...])
    gate_b = proj[:, :c]
    u = proj[:, c:2 * c] * proj[:, 2 * c:3 * c]
    qm = proj[:, 3 * c:]

    @pl.when(i == 0)
    def _():
        ubuf[SUBLANES - 2:SUBLANES, :] = prev_ref[0]

    @pl.when(i > 0)
    def _():
        ubuf[SUBLANES - 2:SUBLANES, :] = ubuf[tm + SUBLANES - 2:tm + SUBLANES, :]

    ubuf[SUBLANES:SUBLANES + tm, :] = u
    cw = cw_ref[...]
    y = cw[0:1] * ubuf[SUBLANES - 2:SUBLANES - 2 + tm, :]
    y = y + cw[1:2] * ubuf[SUBLANES - 1:SUBLANES - 1 + tm, :]
    y = y + cw[2:3] * u
    mix = gate_b * y
    mo = _mem_attn(qm, mk_ref[0], mv_ref[0])
    out = _dot(mix.astype(BF16), wout_ref[:c, :]) + _dot(mo.astype(BF16), wout_ref[c:, :])
    h_ref[...] = x + out

    @pl.when(i == n_t - 1)
    def _():
        st_ref[0] = ubuf[tm + SUBLANES - 2:tm + SUBLANES, :]


def _mem_spec(n_mem):
    return pl.BlockSpec((1, MEM_W, n_mem), lambda bi, i: (bi, 0, 0))


def _mixa(x2d, b, t, tm, g, win_bf, conv_w, wout_bf, mem_k, mem_v, prev):
    d = x2d.shape[1]
    c = d - MEM_W
    n_t = t // tm
    n_mem = mem_k.shape[2]
    kern = functools.partial(_mixa_kernel, tm=tm, c=c, n_t=n_t)
    return pl.pallas_call(
        kern,
        grid=(b, n_t),
        in_specs=[
            pl.BlockSpec((tm, d), lambda bi, i: (bi * n_t + i, 0)),
            _resident((1, d)),
            _resident(win_bf.shape),
            _resident(conv_w.shape),
            _resident(wout_bf.shape),
            _mem_spec(n_mem),
            _mem_spec(n_mem),
            pl.BlockSpec((1, CONV_WIDTH - 1, c), lambda bi, i: (bi, 0, 0)),
        ],
        out_specs=[
            pl.BlockSpec((tm, d), lambda bi, i: (bi * n_t + i, 0)),
            pl.BlockSpec((1, CONV_WIDTH - 1, c), lambda bi, i: (bi, 0, 0)),
        ],
        out_shape=[
            jax.ShapeDtypeStruct(x2d.shape, F32),
            jax.ShapeDtypeStruct((b, CONV_WIDTH - 1, c), F32),
        ],
        scratch_shapes=[pltpu.VMEM((tm + SUBLANES, c), F32)],
        compiler_params=_cparams("arbitrary", "arbitrary"),
        name="mixa",
    )(x2d, g.reshape(1, d), win_bf, conv_w, wout_bf, mem_k, mem_v, prev)


def _mixa_short_kernel(x_ref, g_ref, win_ref, cw_ref, wout_ref, mk_ref, mv_ref, prev_ref, h_ref, st_ref,
                       *, c, n_seq, t):
    x = x_ref[...]
    hn = _rms(x, g_ref[...]).astype(BF16)
    proj = _dot(hn, win_ref[...])
    cw = cw_ref[...]
    row = lax.broadcasted_iota(jnp.int32, (t, c), 0)
    mixed = []
    for s in range(n_seq):
        p = proj[s * t:(s + 1) * t]
        u = p[:, c:2 * c] * p[:, 2 * c:3 * c]
        prev = prev_ref[s]
        back1 = jnp.where(row == 0, prev[1:2], pltpu.roll(u, 1, 0))
        back2 = jnp.where(row == 0, prev[0:1], jnp.where(row == 1, prev[1:2], pltpu.roll(u, 2, 0)))
        y = cw[0:1] * back2 + cw[1:2] * back1 + cw[2:3] * u
        mo = _mem_attn(p[:, 3 * c:], mk_ref[s], mv_ref[s])
        mixed.append(jnp.concatenate([p[:, :c] * y, mo], axis=1))
        st_ref[s] = u[t - (CONV_WIDTH - 1):]
    h_ref[...] = x + _dot(jnp.concatenate(mixed, axis=0).astype(BF16), wout_ref[...])


def _mixa_short(x2d, b, t, n_seq, g, win_bf, conv_w, wout_bf, mem_k, mem_v, prev):
    d = x2d.shape[1]
    c = d - MEM_W
    n_mem = mem_k.shape[2]
    kern = functools.partial(_mixa_short_kernel, c=c, n_seq=n_seq, t=t)
    seqs = lambda i: (i, 0, 0)
    return pl.pallas_call(
        kern,
        grid=(b // n_seq,),
        in_specs=[
            pl.BlockSpec((n_seq * t, d), lambda i: (i, 0)),
            _resident((1, d)),
            _resident(win_bf.shape),
            _resident(conv_w.shape),
            _resident(wout_bf.shape),
            pl.BlockSpec((n_seq, MEM_W, n_mem), seqs),
            pl.BlockSpec((n_seq, MEM_W, n_mem), seqs),
            pl.BlockSpec((n_seq, CONV_WIDTH - 1, c), seqs),
        ],
        out_specs=[pl.BlockSpec((n_seq * t, d), lambda i: (i, 0)), pl.BlockSpec((n_seq, CONV_WIDTH - 1, c), seqs)],
        out_shape=[jax.ShapeDtypeStruct(x2d.shape, F32), jax.ShapeDtypeStruct((b, CONV_WIDTH - 1, c), F32)],
        compiler_params=_cparams("arbitrary"),
        name="mixa_short",
    )(x2d, g.reshape(1, d), win_bf, conv_w, wout_bf, mem_k, mem_v, prev)


def _silu(g):
    return g / (1.0 + jnp.exp(-g))


def _ffn_kernel(h_ref, g_ref, wg_ref, wu_ref, wd_ref, o_ref):
    h = h_ref[...]
    hn = _rms(h, g_ref[...]).astype(BF16)
    a = _silu(_dot(hn, wg_ref[...])) * _dot(hn, wu_ref[...])
    o_ref[...] = h + _dot(a.astype(BF16), wd_ref[...])


def _ffn(h2d, tm, g, wg_bf, wu_bf, wd_bf):
    m, d = h2d.shape
    return pl.pallas_call(
        _ffn_kernel,
        grid=(m // tm,),
        in_specs=[
            pl.BlockSpec((tm, d), lambda i: (i, 0)),
            _resident((1, d)),
            _resident(wg_bf.shape),
            _resident(wu_bf.shape),
            _resident(wd_bf.shape),
        ],
        out_specs=pl.BlockSpec((tm, d), lambda i: (i, 0)),
        out_shape=jax.ShapeDtypeStruct(h2d.shape, F32),
        compiler_params=_cparams("arbitrary"),
        name="ffn",
    )(h2d, g.reshape(1, d), wg_bf, wu_bf, wd_bf)


def _kvq_kernel(h_ref, gkv_ref, gq_ref, wkv_ref, wq_ref, cos_ref, sin_ref,
                k_ref, v_ref, kb_ref, vb_ref, q_ref, qm_ref, *, n_heads):
    x = h_ref[...]
    xn = x * lax.rsqrt(jnp.mean(x * x, axis=-1, keepdims=True) + EPS)
    cos = cos_ref[...]
    sin = sin_ref[...]
    kv = _dot((xn * gkv_ref[...]).astype(BF16), wkv_ref[...])
    for h in range(n_heads):
        r = _rope_slab(kv[:, h * HEAD_W:(h + 1) * HEAD_W], cos, sin)
        k_ref[0, h] = r
        kb_ref[0, h] = r.astype(BF16)
        v = kv[:, (n_heads + h) * HEAD_W:(n_heads + h + 1) * HEAD_W]
        v_ref[0, h] = v
        vb_ref[0, h] = v.astype(BF16)
    p = _dot((xn * gq_ref[...]).astype(BF16), wq_ref[...])
    for h in range(n_heads):
        q_ref[0, h] = (_rope_slab(p[:, h * HEAD_W:(h + 1) * HEAD_W], cos, sin) * QK_SCALE).astype(BF16)
    qm_ref[...] = p[:, n_heads * HEAD_W:]


def _rope_tables(pos):
    inv = ROPE_THETA ** (-jnp.arange(ROPE_HALF, dtype=F32) / ROPE_HALF)
    ang = pos.astype(F32)[:, None] * inv[None, :]
    cos = jnp.tile(jnp.cos(ang), (1, HEAD_W // ROPE_HALF))
    sin = jnp.sin(ang)
    sin = jnp.tile(jnp.concatenate([-sin, sin], axis=-1), (1, HEAD_W // HALF_W))
    return cos, sin


def _kvqproj(h2d, b, t, tm, g_kv, wkv_bf, g_q, wq_bf, cos, sin):
    m, d = h2d.shape
    n_heads = wkv_bf.shape[1] // (2 * HEAD_W)
    n_t = t // tm
    kern = functools.partial(_kvq_kernel, n_heads=n_heads)
    head_major = pl.BlockSpec((1, n_heads, tm, HEAD_W), lambda bi, i: (bi, 0, i, 0))
    rope = pl.BlockSpec((tm, HEAD_W), lambda bi, i: (i, 0))
    rows = lambda bi, i: (bi * n_t + i, 0)
    hm_f32 = jax.ShapeDtypeStruct((b, n_heads, t, HEAD_W), F32)
    hm_bf16 = jax.ShapeDtypeStruct((b, n_heads, t, HEAD_W), BF16)
    return pl.pallas_call(
        kern,
        grid=(b, n_t),
        in_specs=[pl.BlockSpec((tm, d), rows), _resident((1, d)), _resident((1, d)), _resident(wkv_bf.shape),
                  _resident(wq_bf.shape), rope, rope],
        out_specs=[head_major] * 5 + [pl.BlockSpec((tm, MEM_W), rows)],
        out_shape=[hm_f32, hm_f32, hm_bf16, hm_bf16, hm_bf16, jax.ShapeDtypeStruct((m, MEM_W), F32)],
        compiler_params=_cparams("arbitrary", "arbitrary"),
        name="kvqproj",
    )(h2d, g_kv.reshape(1, d), g_q.reshape(1, d), wkv_bf, wq_bf, cos, sin)


def _stack_components(q):
    lane = lax.broadcasted_iota(jnp.int32, q.shape, 1)
    zero = jnp.zeros_like(q)
    return jnp.concatenate([jnp.where(lane < HALF_W, q, zero), jnp.where(lane >= HALF_W, q, zero)], axis=0)


def _online_update(s, pv, m_ref, l_ref, acc_ref):
    cols = [s[:, c * LANES:(c + 1) * LANES] for c in range(s.shape[1] // LANES)]
    m_prev = m_ref[...]
    m_new = jnp.maximum(m_prev, jnp.max(functools.reduce(jnp.maximum, cols), axis=-1, keepdims=True))
    alpha = jnp.exp2(m_prev - m_new)
    ps = [jnp.exp2(c - m_new) for c in cols]
    l_ref[...] = alpha * l_ref[...] + functools.reduce(jnp.add, ps)
    acc_ref[...] = alpha * acc_ref[...] + pv(jnp.concatenate(ps, axis=1).astype(BF16))
    m_ref[...] = m_new


def _init_online(m_ref, l_ref, acc_ref):
    m_ref[...] = jnp.full(m_ref.shape, NEG, F32)
    l_ref[...] = jnp.zeros(l_ref.shape, F32)
    acc_ref[...] = jnp.zeros(acc_ref.shape, F32)


def _diff_combine(l_ref, acc_ref, t, lam):
    o = acc_ref[...] / jnp.sum(l_ref[...], axis=-1, keepdims=True)
    return o[:t] - lam * o[t:]


def _attn_prompt_kernel(q_ref, k_ref, v_ref, lam_ref, sg_ref, o_ref, m_ref, l_ref, acc_ref, *, tq, lam_init):
    qi = pl.program_id(2)
    qs = _stack_components(q_ref[0, 0])
    _init_online(m_ref, l_ref, acc_ref)

    def chunk(j, masked):
        start = pl.multiple_of(j * tq, tq)
        s = _dot_t(qs, k_ref[0, 0, pl.ds(start, tq), :])
        if masked:
            row = lax.broadcasted_iota(jnp.int32, s.shape, 0) % tq
            col = lax.broadcasted_iota(jnp.int32, s.shape, 1)
            s = jnp.where(col <= row, s, NEG)
        v = v_ref[0, 0, pl.ds(start, tq), :]
        _online_update(s, lambda p: _dot(p, v), m_ref, l_ref, acc_ref)

    def body(j, carry):
        chunk(2 * j, False)
        chunk(2 * j + 1, False)
        return carry

    lax.fori_loop(0, qi // 2, body, 0)

    @pl.when(qi % 2 == 1)
    def _():
        chunk(qi - 1, False)

    chunk(qi, True)
    o = _diff_combine(l_ref, acc_ref, tq, _lam(lam_ref, lam_init))
    o_ref[...] = _subln(o, sg_ref[...], lam_init).astype(BF16)


def _attn_prompt(q_bf, k_bf, v_bf, tq, lam_p, subln_g, lam_init):
    b, n_heads, t, _ = q_bf.shape
    nq = t // tq
    kern = functools.partial(_attn_prompt_kernel, tq=tq, lam_init=lam_init)
    seq = pl.BlockSpec((1, 1, t, HEAD_W), lambda bi, h, qi: (bi, h, 0, 0))
    return pl.pallas_call(
        kern,
        grid=(b, n_heads, nq),
        in_specs=[
            pl.BlockSpec((1, 1, tq, HEAD_W), lambda bi, h, qi: (bi, h, qi, 0)),
            seq,
            seq,
            pl.BlockSpec(lam_p.shape, lambda bi, h, qi: (0, 0)),
            pl.BlockSpec((1, HEAD_W), lambda bi, h, qi: (0, 0)),
        ],
        out_specs=pl.BlockSpec((tq, HEAD_W), lambda bi, h, qi: (bi * nq + qi, h)),
        out_shape=jax.ShapeDtypeStruct((b * t, n_heads * HEAD_W), BF16),
        scratch_shapes=[pltpu.VMEM((2 * tq, LANES), F32)] * 3,
        compiler_params=_cparams("arbitrary", "arbitrary", "arbitrary"),
        name="attn_prompt",
    )(q_bf, k_bf, v_bf, lam_p, subln_g.reshape(1, HEAD_W))


def _attn_sample_kernel(pt_ref, q_ref, kn_ref, vn_ref, *rest, ts, n_heads, pages, n_groups, lam_init):
    k_refs = rest[:pages]
    v_refs = rest[pages:2 * pages]
    lam_ref, sg_ref, o_ref, qs_ref, m_ref, l_ref, acc_ref = rest[2 * pages:]
    g = pl.program_id(1)

    r = 2 * ts

    @pl.when(g == 0)
    def _():
        for h in range(n_heads):
            qs_ref[h] = _stack_components(q_ref[0, h])
        _init_online(m_ref, l_ref, acc_ref)

    def all_heads(keys, values):
        s = jnp.concatenate([_dot_t(qs_ref[h], keys(h)) for h in range(n_heads)], axis=0)

        def pv(p):
            return jnp.concatenate([_dot(p[h * r:(h + 1) * r], values(h)) for h in range(n_heads)], axis=0)

        return s, pv

    def past(refs):
        return lambda h: jnp.concatenate([x[0, h].astype(BF16) for x in refs], axis=0)

    s, pv = all_heads(past(k_refs), past(v_refs))
    _online_update(s, pv, m_ref, l_ref, acc_ref)

    @pl.when(g == n_groups - 1)
    def _():
        pad = jnp.zeros((LANES - ts, HEAD_W), BF16)

        def new(ref):
            return lambda h: jnp.concatenate([ref[0, h].astype(BF16), pad], axis=0)

        s, pv = all_heads(new(kn_ref), new(vn_ref))
        row = lax.broadcasted_iota(jnp.int32, s.shape, 0) % ts
        col = lax.broadcasted_iota(jnp.int32, s.shape, 1)
        _online_update(jnp.where(col <= row, s, NEG), pv, m_ref, l_ref, acc_ref)
        lam = _lam(lam_ref, lam_init)
        for h in range(n_heads):
            rows = pl.ds(h * r, r)
            o = _diff_combine(l_ref.at[rows], acc_ref.at[rows], ts, lam)
            o_ref[:, h * HEAD_W:(h + 1) * HEAD_W] = _subln(o, sg_ref[...], lam_init).astype(BF16)


def _attn_sample(q_bf, k_new, v_new, cache_k, cache_v, page_table, pages, lam_p, subln_g, lam_init):
    bs, n_heads, ts, _ = q_bf.shape
    ps = cache_k.shape[2]
    n_groups = page_table.shape[1] // pages
    kern = functools.partial(_attn_sample_kernel, ts=ts, n_heads=n_heads, pages=pages, n_groups=n_groups,
                             lam_init=lam_init)

    def page_spec(i):
        return pl.BlockSpec((1, n_heads, ps, HEAD_W), lambda bi, g, pt: (pt[bi, g * pages + i], 0, 0, 0))

    new_rows = pl.BlockSpec((1, n_heads, ts, HEAD_W), lambda bi, g, pt: (bi, 0, 0, 0))
    grid_spec = pltpu.PrefetchScalarGridSpec(
        num_scalar_prefetch=1,
        grid=(bs, n_groups),
        in_specs=[new_rows] * 3
        + [page_spec(i) for i in range(pages)] * 2
        + [pl.BlockSpec(lam_p.shape, lambda bi, g, pt: (0, 0)), pl.BlockSpec((1, HEAD_W), lambda bi, g, pt: (0, 0))],
        out_specs=pl.BlockSpec((ts, n_heads * HEAD_W), lambda bi, g, pt: (bi, 0)),
        scratch_shapes=[pltpu.VMEM((n_heads, 2 * ts, HEAD_W), BF16)]
        + [pltpu.VMEM((n_heads * 2 * ts, LANES), F32)] * 3,
    )
    return pl.pallas_call(
        kern,
        grid_spec=grid_spec,
        out_shape=jax.ShapeDtypeStruct((bs * ts, n_heads * HEAD_W), BF16),
        compiler_params=_cparams("arbitrary", "arbitrary"),
        name="attn_sample",
    )(page_table, q_bf, k_new, v_new, *([cache_k] * pages), *([cache_v] * pages), lam_p, subln_g.reshape(1, HEAD_W))


def _route(h, g, r_hi, r_lo, base, n_experts):
    hn = _rms(h, g)
    hi = hn.astype(BF16)
    lo = (hn - hi.astype(F32)).astype(BF16)
    logits = _dot(hi, r_hi) + (_dot(hi, r_lo) + _dot(lo, r_hi))
    lane = lax.broadcasted_iota(jnp.int32, logits.shape, 1)
    logits = jnp.where(lane < n_experts, logits, -jnp.inf)
    v1 = jnp.max(logits, axis=-1, keepdims=True)
    e1 = jnp.min(jnp.where(logits == v1, lane, LANES), axis=-1, keepdims=True)
    rest = jnp.where(lane == e1, -jnp.inf, logits)
    v2 = jnp.max(rest, axis=-1, keepdims=True)
    e2 = jnp.min(jnp.where(rest == v2, lane, LANES), axis=-1, keepdims=True)
    ex = jnp.exp(v2 - v1)
    g1 = 1.0 / (1.0 + ex)
    g2 = ex / (1.0 + ex)
    sel = jnp.where((lane == e1) | (lane == e2), 1.0, 0.0)
    t = h.shape[0]
    earlier = lax.broadcasted_iota(jnp.int32, (t, t), 0) > lax.broadcasted_iota(jnp.int32, (t, t), 1)
    cum = _dot(jnp.where(earlier, 1.0, 0.0).astype(BF16), sel.astype(BF16)) + base
    r1 = jnp.sum(jnp.where(lane == e1, cum, 0.0), axis=-1, keepdims=True)
    r2 = jnp.sum(jnp.where(lane == e2, cum, 0.0), axis=-1, keepdims=True)
    out = jnp.where(lane == 0, e1.astype(F32), 0.0)
    for i, val in enumerate((e2.astype(F32), g1, g2, r1, r2), start=1):
        out = jnp.where(lane == i, val, out)
    return out, sel


def _mixb_kernel(mix_ref, qm_ref, x_ref, wout_ref, mk_ref, mv_ref, gf_ref, rhi_ref, rlo_ref,
                 h_ref, route_ref, cnt_ref, carry, *, n_experts, n_seq):
    @pl.when((pl.program_id(0) == 0) & (pl.program_id(1) == 0))
    def _():
        carry[...] = jnp.zeros(carry.shape, F32)

    c = mix_ref.shape[1]
    qm = qm_ref[...]
    t = qm.shape[0] // n_seq
    mo = jnp.concatenate([_mem_attn(qm[s * t:(s + 1) * t], mk_ref[s], mv_ref[s]) for s in range(n_seq)], axis=0)
    h = x_ref[...] + (_dot(mix_ref[...], wout_ref[:c, :]) + _dot(mo.astype(BF16), wout_ref[c:, :]))
    h_ref[...] = h
    route, sel = _route(h, gf_ref[...], rhi_ref[...], rlo_ref[...], carry[...], n_experts)
    route_ref[...] = route
    carry[...] += jnp.sum(sel, axis=0, keepdims=True)
    cnt_ref[...] = carry[...]


def _mixb(mix_bf, qm, x2d, b, t, tm, wout_bf, mem_k, mem_v, g_ffn, router, n_seq=1):
    m, d = x2d.shape
    c = mix_bf.shape[1]
    assert n_seq == 1 or tm == t
    n_t = t // tm
    tm = tm * n_seq
    n_mem = mem_k.shape[2]
    n_experts = router.shape[1]
    r_pad = jnp.zeros((d, LANES), F32).at[:, :n_experts].set(router)
    r_hi = r_pad.astype(BF16)
    r_lo = (r_pad - r_hi.astype(F32)).astype(BF16)
    row = lambda bi, i: (bi * n_t + i, 0)
    mem = pl.BlockSpec((n_seq, MEM_W, n_mem), lambda bi, i: (bi, 0, 0))
    kern = functools.partial(_mixb_kernel, n_experts=n_experts, n_seq=n_seq)
    return pl.pallas_call(
        kern,
        grid=(b // n_seq, n_t),
        in_specs=[
            pl.BlockSpec((tm, c), row),
            pl.BlockSpec((tm, MEM_W), row),
            pl.BlockSpec((tm, d), row),
            _resident(wout_bf.shape),
            mem,
            mem,
            _resident((1, d)),
            _resident((d, LANES)),
            _resident((d, LANES)),
        ],
        out_specs=[
            pl.BlockSpec((tm, d), row),
            pl.BlockSpec((tm, LANES), row),
            pl.BlockSpec((1, LANES), lambda bi, i: (0, 0)),
        ],
        out_shape=[
            jax.ShapeDtypeStruct((m, d), F32),
            jax.ShapeDtypeStruct((m, LANES), F32),
            jax.ShapeDtypeStruct((1, LANES), F32),
        ],
        scratch_shapes=[pltpu.VMEM((1, LANES), F32)],
        compiler_params=_cparams("arbitrary", "arbitrary"),
        name="mixb",
    )(mix_bf, qm, x2d, wout_bf, mem_k, mem_v, g_ffn.reshape(1, d), r_hi, r_lo)


GATHER_UNROLL = 8
SEQS_PER_STEP = 8


def _row_copy(src_hbm, row, dst, slot, sem):
    return pltpu.make_async_copy(src_hbm.at[pl.ds(row, 1)], dst.at[pl.ds(slot, 1)], sem)


def _gather_start(src_hbm, idx_ref, base, dst, sem, n):
    def issue(j, carry):
        for u in range(GATHER_UNROLL):
            r = j * GATHER_UNROLL + u
            _row_copy(src_hbm, idx_ref[base + r], dst, r, sem).start(priority=u % 2)
        return carry

    lax.fori_loop(0, n // GATHER_UNROLL, issue, 0)


def _gather_start_inline(src_hbm, idx_ref, base, dst, sem, n):
    for r in range(n):
        _row_copy(src_hbm, idx_ref[base + r], dst, r, sem).start(priority=r % 2)


def _gather_wait(src_hbm, dst, sem, n):
    def wait(r, carry):
        _row_copy(src_hbm, 0, dst, r, sem).wait()
        return carry

    lax.fori_loop(0, n, wait, 0, unroll=GATHER_UNROLL)


def _expert_kernel(blk_e_ref, tok_ref, nused_ref, h_hbm, g_ref, wg_ref, wu_ref, wd_ref, o_ref, xbuf, sem, *, blk):
    i = pl.program_id(0)
    n_used = nused_ref[0]
    slot = i % 2

    @pl.when(i == 0)
    def _():
        _gather_start(h_hbm, tok_ref, 0, xbuf.at[0], sem.at[0], blk)

    @pl.when(i + 1 < n_used)
    def _():
        _gather_start(h_hbm, tok_ref, (i + 1) * blk, xbuf.at[1 - slot], sem.at[1 - slot], blk)

    @pl.when(i < n_used)
    def _():
        _gather_wait(h_hbm, xbuf.at[slot], sem.at[slot], blk)
        hn = _rms(xbuf[slot], g_ref[...]).astype(BF16)
        a = _silu(_dot(hn, wg_ref[0])) * _dot(hn, wu_ref[0])
        o_ref[...] = _dot(a.astype(BF16), wd_ref[0])

    @pl.when(i >= n_used)
    def _():
        o_ref[...] = jnp.zeros(o_ref.shape, F32)


def _experts(h2d, g, wg_bf, wu_bf, wd_bf, blk_e, buf_tok, n_used, blk):
    n_rows = buf_tok.shape[0]
    n_blk = n_rows // blk
    _, d, ff = wg_bf.shape
    kern = functools.partial(_expert_kernel, blk=blk)
    grid_spec = pltpu.PrefetchScalarGridSpec(
        num_scalar_prefetch=3,
        grid=(n_blk,),
        in_specs=[
            pl.BlockSpec(memory_space=pl.ANY),
            pl.BlockSpec((1, d), lambda i, be, tk, nu: (0, 0)),
            pl.BlockSpec((1, d, ff), lambda i, be, tk, nu: (be[i], 0, 0)),
            pl.BlockSpec((1, d, ff), lambda i, be, tk, nu: (be[i], 0, 0)),
            pl.BlockSpec((1, ff, d), lambda i, be, tk, nu: (be[i], 0, 0)),
        ],
        out_specs=pl.BlockSpec((blk, d), lambda i, be, tk, nu: (i, 0)),
        scratch_shapes=[pltpu.VMEM((2, blk, d), F32), pltpu.SemaphoreType.DMA((2,))],
    )
    return pl.pallas_call(
        kern,
        grid_spec=grid_spec,
        out_shape=jax.ShapeDtypeStruct((n_rows, d), F32),
        compiler_params=_cparams("arbitrary"),
        name="experts",
    )(blk_e, buf_tok, n_used, h2d, g.reshape(1, d), wg_bf, wu_bf, wd_bf)


def _combine_kernel(pos_ref, h_ref, route_ref, g_ref, yb_hbm, o_ref, ybuf, sem, *, tc, n_steps):
    i = pl.program_id(0)
    slot = i % 2

    @pl.when(i == 0)
    def _():
        for k in range(TOP_K):
            _gather_start(yb_hbm, pos_ref, k * n_steps * tc, ybuf.at[0, k], sem.at[0], tc)

    for k in range(TOP_K):
        _gather_wait(yb_hbm, ybuf.at[slot, k], sem.at[slot], tc)
    nxt = jnp.minimum(i + 1, n_steps - 1)
    for k in range(TOP_K):
        _gather_start_inline(yb_hbm, pos_ref, (k * n_steps + nxt) * tc, ybuf.at[1 - slot, k], sem.at[1 - slot], tc)
    route = route_ref[...]
    y = ybuf[slot, 0] * route[:, 2:3] + ybuf[slot, 1] * route[:, 3:4]
    o_ref[...] = _rms(h_ref[...] + y, g_ref[...])

    @pl.when(i == n_steps - 1)
    def _():
        for k in range(TOP_K):
            _gather_wait(yb_hbm, ybuf.at[1 - slot, k], sem.at[1 - slot], tc)


def _combine(h2d, route, yb, pos_k_major, g, tc):
    m, d = h2d.shape
    n_steps = m // tc
    kern = functools.partial(_combine_kernel, tc=tc, n_steps=n_steps)
    grid_spec = pltpu.PrefetchScalarGridSpec(
        num_scalar_prefetch=1,
        grid=(n_steps,),
        in_specs=[
            pl.BlockSpec((tc, d), lambda i, p: (i, 0)),
            pl.BlockSpec((tc, LANES), lambda i, p: (i, 0)),
            pl.BlockSpec((1, d), lambda i, p: (0, 0)),
            pl.BlockSpec(memory_space=pl.ANY),
        ],
        out_specs=pl.BlockSpec((tc, d), lambda i, p: (i, 0)),
        scratch_shapes=[pltpu.VMEM((2, TOP_K, tc, d), F32), pltpu.SemaphoreType.DMA((2,))],
    )
    return pl.pallas_call(
        kern,
        grid_spec=grid_spec,
        out_shape=jax.ShapeDtypeStruct((m, d), F32),
        compiler_params=_cparams("arbitrary"),
        name="combine",
    )(pos_k_major, h2d, route, g.reshape(1, d), yb)


def _slot_tokens_kernel(dest_ref, pad_lo_ref, pad_hi_ref, o_ref, *, n_assign, n_pads):
    def zero(p, carry):
        o_ref[p] = 0
        return carry

    for e in range(n_pads):
        lax.fori_loop(pad_lo_ref[e], pad_hi_ref[e], zero, 0)

    def put(j, carry):
        for u in range(GATHER_UNROLL):
            tok = j * GATHER_UNROLL + u
            for k in range(TOP_K):
                o_ref[dest_ref[tok * TOP_K + k]] = tok
        return carry

    lax.fori_loop(0, n_assign // (TOP_K * GATHER_UNROLL), put, 0)


def _slot_tokens(dest_flat, pad_lo, pad_hi, n_rows):
    n_assign = dest_flat.shape[0]
    kern = functools.partial(_slot_tokens_kernel, n_assign=n_assign, n_pads=pad_lo.shape[0])
    grid_spec = pltpu.PrefetchScalarGridSpec(
        num_scalar_prefetch=3, grid=(1,), in_specs=[], out_specs=pl.BlockSpec(memory_space=pltpu.SMEM))
    return pl.pallas_call(
        kern,
        grid_spec=grid_spec,
        out_shape=jax.ShapeDtypeStruct((n_rows,), jnp.int32),
        compiler_params=_cparams("arbitrary"),
        name="slot_tokens",
    )(dest_flat, pad_lo, pad_hi)


def _moe(h2d, route, counts, tc, g_ffn, wg_bf, wu_bf, wd_bf, g_final, blk):
    m, d = h2d.shape
    n_experts = wg_bf.shape[0]
    a = m * TOP_K
    counts = counts[0, :n_experts].astype(jnp.int32)
    padded = (counts + blk - 1) // blk * blk
    cum_pad = jnp.cumsum(padded)
    top_e = route[:, :TOP_K].astype(jnp.int32)
    dest = (cum_pad - padded)[top_e] + route[:, 2 * TOP_K:3 * TOP_K].astype(jnp.int32)
    n_blk = -(-(a + n_experts * (blk - 1)) // blk)
    n_rows = n_blk * blk
    pad_lo = jnp.concatenate([cum_pad - padded + counts, cum_pad[-1:]]).astype(jnp.int32)
    pad_hi = jnp.concatenate([cum_pad, jnp.full((1,), n_rows, cum_pad.dtype)]).astype(jnp.int32)
    buf_tok = _slot_tokens(dest.reshape(-1), pad_lo, pad_hi, n_rows)
    n_used = (cum_pad[-1] // blk).astype(jnp.int32)
    blk_i = jnp.minimum(jnp.arange(n_blk, dtype=jnp.int32), n_used - 1)
    blk_e = jnp.sum((blk_i[:, None] * blk >= cum_pad[None, :]).astype(jnp.int32), axis=1)
    blk_e = jnp.minimum(blk_e, n_experts - 1)
    yb = _experts(h2d, g_ffn, wg_bf, wu_bf, wd_bf, blk_e, buf_tok, n_used.reshape(1), blk)
    return _combine(h2d, route, yb, dest.T.reshape(-1), g_final, tc)


def _head_major(x, b, t):
    n_heads = x.shape[1]
    return jnp.transpose(x.reshape(n_heads, b, t, HEAD_W), (1, 0, 2, 3))


def _trunk(x, pos, conv_prev, mem_k, mem_v, attend, w, tm_a, tm, tok_tm, moe_blk):
    b, t, d = x.shape
    x2d = x.reshape(b * t, d)
    short = t == SUBLANES
    n_seq = math.gcd(b, SEQS_PER_STEP) if short else 1
    mix_a = (w['g_mix'][0], w['a_w_in'], w['a_conv'], w['a_w_out'], mem_k[0], mem_v[0], conv_prev)
    h, conv_st = _mixa_short(x2d, b, t, n_seq, *mix_a) if short else _mixa(x2d, b, t, tm_a, *mix_a)
    h = _ffn(h, tok_tm, w['g_ffn'][0], w['f_w_gate'], w['f_w_up'], w['f_w_down'])
    cos, sin = _rope_tables(pos)
    proj = (w['g_kv'], w['w_kv'], w['g_mix'][1], w['b_w_in'])
    if tok_tm > t:
        cos, sin = jnp.tile(cos, (b, 1)), jnp.tile(sin, (b, 1))
        *heads, qm = _kvqproj(h, 1, b * t, tok_tm, *proj, cos, sin)
        k, v, k_bf, v_bf, q_bf = (_head_major(o, b, t) for o in heads)
    else:
        k, v, k_bf, v_bf, q_bf, qm = _kvqproj(h, b, t, tok_tm, *proj, cos, sin)
    mix = attend(q_bf, k, v, k_bf, v_bf)
    h, route, counts = _mixb(mix, qm, h, b, t, tm, w['b_w_out'], mem_k[1], mem_v[1], w['g_ffn'][1], w['m_router'],
                             n_seq)
    y = _moe(h, route, counts, tok_tm, w['g_ffn'][1], w['m_w_gate'], w['m_w_up'], w['m_w_down'], w['g_final'],
             moe_blk)
    return y.reshape(b, t, d), conv_st, k, v


def _token_major(x):
    return jnp.transpose(x, (0, 2, 1, 3))


def kernel(x_prompt, x_sample, state_conv, cache_k, cache_v, cache_mem_k, cache_mem_v, page_table, mem_prompt, g_mix, g_ffn, g_mem, w_mem_kv, a_w_in, a_conv, a_w_out, g_kv, w_kv, b_w_in, b_lambda, b_subln, b_w_out, f_w_gate, f_w_up, f_w_down, m_router, m_w_gate, m_w_up, m_w_down, g_final):
    bp, tp, d = x_prompt.shape
    bs, ts, _ = x_sample.shape
    depth = g_mix.shape[0]
    n_a = a_w_in.shape[0]
    assert depth == 2 and n_a == 1 and b_w_in.shape[0] == 1, "one conv layer followed by one attention layer"
    assert ts == SUBLANES and tp % 512 == 0
    c = d - MEM_W
    lam_init = 0.8 - 0.6 * math.exp(-0.3 * n_a)
    w = {
        'g_mix': g_mix, 'g_ffn': g_ffn, 'g_kv': g_kv, 'g_final': g_final,
        'a_w_in': a_w_in[0].astype(BF16), 'a_conv': a_conv[0], 'a_w_out': a_w_out[0].astype(BF16),
        'w_kv': w_kv.astype(BF16), 'b_w_in': b_w_in[0].astype(BF16), 'b_w_out': b_w_out[0].astype(BF16),
        'f_w_gate': f_w_gate[0].astype(BF16), 'f_w_up': f_w_up[0].astype(BF16), 'f_w_down': f_w_down[0].astype(BF16),
        'm_router': m_router[0], 'm_w_gate': m_w_gate[0].astype(BF16), 'm_w_up': m_w_up[0].astype(BF16),
        'm_w_down': m_w_down[0].astype(BF16),
    }
    lam_p = b_lambda[0]
    subln_g = b_subln[0]

    n_mem = mem_prompt.shape[1]
    mem_kt, mem_vt = _memkv(mem_prompt, g_mem, w_mem_kv.astype(BF16))

    def mem_out(x):
        return jnp.transpose(x.reshape(depth, bp, MEM_HEADS, MEM_HEAD_DIM, n_mem), (0, 1, 4, 2, 3))

    def attend_prompt(q_bf, k, v, k_bf, v_bf):
        return _attn_prompt(q_bf, k_bf, v_bf, 512, lam_p, subln_g, lam_init)

    y_p, conv_p, k_p, v_p = _trunk(
        x_prompt, jnp.arange(tp), jnp.zeros((bp, CONV_WIDTH - 1, c), F32), mem_kt, mem_vt, attend_prompt, w,
        tm_a=512, tm=256, tok_tm=256, moe_blk=256)

    past = page_table.shape[1] * cache_k.shape[1]
    pages = math.gcd(page_table.shape[1], 16)
    cache_kh = jnp.transpose(cache_k, (0, 2, 1, 3))
    cache_vh = jnp.transpose(cache_v, (0, 2, 1, 3))

    def mem_in(x):
        return jnp.transpose(x, (0, 1, 3, 4, 2)).reshape(depth, bs, MEM_W, x.shape[2])

    def attend_sample(q_bf, k, v, k_bf, v_bf):
        return _attn_sample(q_bf, k, v, cache_kh, cache_vh, page_table, pages, lam_p, subln_g, lam_init)

    y_s, conv_s, k_s, v_s = _trunk(
        x_sample, past + jnp.arange(ts), state_conv[0], mem_in(cache_mem_k), mem_in(cache_mem_v), attend_sample, w,
        tm_a=ts, tm=ts, tok_tm=bs * ts, moe_blk=128)

    return (y_p, y_s, conv_p[None], conv_s[None], _token_major(k_p), _token_major(v_p), _token_major(k_s),
            _token_major(v_s), mem_out(mem_kt), mem_out(mem_vt))
```

```python
import functools
import math

import jax
import jax.numpy as jnp
from jax import lax
from jax.experimental import pallas as pl
from jax.experimental.pallas import tpu as pltpu

F32 = jnp.float32
BF16 = jnp.bfloat16

EPS = 1e-6
MEM_HEADS = 4
MEM_HEAD_DIM = 64
MEM_W = MEM_HEADS * MEM_HEAD_DIM
HEAD_W = 128
HALF_W = HEAD_W // 2
ROPE_HALF = HALF_W // 2
ROPE_THETA = 10000.0
CONV_WIDTH = 3
TOP_K = 2
NEG = -1e30
QK_SCALE = HALF_W ** -0.5 * math.log2(math.e)
MEM_SCALE = MEM_HEAD_DIM ** -0.5
LANES = 128
SUBLANES = 8
VMEM_LIMIT = 56 * 1024 * 1024


def _cparams(*sem):
    return pltpu.CompilerParams(dimension_semantics=sem, vmem_limit_bytes=VMEM_LIMIT)


def _rms(x, g):
    return x * lax.rsqrt(jnp.mean(x * x, axis=-1, keepdims=True) + EPS) * g


def _dot(a, b):
    return jnp.dot(a, b, preferred_element_type=F32)


def _dot_t(a, b):
    return lax.dot_general(a, b, (((1,), (1,)), ((), ())), preferred_element_type=F32)


def _resident(shape):
    n = len(shape)
    return pl.BlockSpec(shape, lambda *_: (0,) * n, pipeline_mode=pl.Buffered(1))


def _mem_attn(qm, mk_t, mv_t):
    q = (qm * MEM_SCALE).astype(BF16)
    kb = mk_t.astype(BF16)
    vb = mv_t.astype(BF16)
    t = q.shape[0]
    lane = lax.broadcasted_iota(jnp.int32, q.shape, 1)
    in_head = [(lane >= h * MEM_HEAD_DIM) & (lane < (h + 1) * MEM_HEAD_DIM) for h in range(MEM_HEADS)]
    s = _dot(jnp.concatenate([jnp.where(sel, q, jnp.zeros_like(q)) for sel in in_head], axis=0), kb)
    m = jnp.max(s, axis=-1, keepdims=True)
    p = jnp.exp(s - m)
    p = p / jnp.sum(p, axis=-1, keepdims=True)
    pv = _dot_t(p.astype(BF16), vb)
    out = jnp.zeros(q.shape, F32)
    for h, sel in enumerate(in_head):
        out = jnp.where(sel, pv[h * t:(h + 1) * t], out)
    return out


def _rope_slab(x, cos, sin_signed):
    lane = lax.broadcasted_iota(jnp.int32, x.shape, 1)
    first_half = (lane % HALF_W) < ROPE_HALF
    partner = jnp.where(first_half, pltpu.roll(x, HEAD_W - ROPE_HALF, 1), pltpu.roll(x, ROPE_HALF, 1))
    return x * cos + partner * sin_signed


def _lam(lam_ref, lam_init):
    lp = lam_ref[...]
    a = jnp.sum(lp[0:1] * lp[1:2], axis=-1, keepdims=True)
    b = jnp.sum(lp[2:3] * lp[3:4], axis=-1, keepdims=True)
    return jnp.exp(a) - jnp.exp(b) + lam_init


def _subln(o, g, lam_init):
    o = o * lax.rsqrt(jnp.mean(o * o, axis=-1, keepdims=True) + EPS)
    return o * g * (1.0 - lam_init)


def _memkv_kernel(x_ref, g_ref, w_ref, k_ref, v_ref):
    hn = _rms(x_ref[0], g_ref[0]).astype(BF16)
    kv_t = _dot(hn, w_ref[0]).T
    k_ref[0, 0] = kv_t[:MEM_W]
    v_ref[0, 0] = kv_t[MEM_W:]


def _memkv(mem, g_mem, w_bf):
    depth, d, n = w_bf.shape
    b, n_mem, _ = mem.shape
    out = jax.ShapeDtypeStruct((depth, b, MEM_W, n_mem), F32)
    return pl.pallas_call(
        _memkv_kernel,
        grid=(depth, b),
        in_specs=[
            pl.BlockSpec((1, n_mem, d), lambda l, bi: (bi, 0, 0)),
            pl.BlockSpec((1, 1, d), lambda l, bi: (l, 0, 0)),
            pl.BlockSpec((1, d, n), lambda l, bi: (l, 0, 0)),
        ],
        out_specs=[pl.BlockSpec((1, 1, MEM_W, n_mem), lambda l, bi: (l, bi, 0, 0))] * 2,
        out_shape=[out, out],
        compiler_params=_cparams("arbitrary", "arbitrary"),
        name="memkv",
    )(mem, g_mem.reshape(depth, 1, d), w_bf)


def _mixa_kernel(x_ref, g_ref, win_ref, cw_ref, wout_ref, mk_ref, mv_ref, prev_ref, h_ref, st_ref, ubuf,
                 *, tm, c, n_t):
    i = pl.program_id(1)
    x = x_ref[...]
    hn = _rms(x, g_ref[...]).astype(BF16)
    proj = _dot(hn, win_ref[...])
    gate_b = proj[:, :c]
    u = proj[:, c:2 * c] * proj[:, 2 * c:3 * c]
    qm = proj[:, 3 * c:]

    @pl.when(i == 0)
    def _():
        ubuf[SUBLANES - 2:SUBLANES, :] = prev_ref[0]

    @pl.when(i > 0)
    def _():
        ubuf[SUBLANES - 2:SUBLANES, :] = ubuf[tm + SUBLANES - 2:tm + SUBLANES, :]

    ubuf[SUBLANES:SUBLANES + tm, :] = u
    cw = cw_ref[...]
    y = cw[0:1] * ubuf[SUBLANES - 2:SUBLANES - 2 + tm, :]
    y = y + cw[1:2] * ubuf[SUBLANES - 1:SUBLANES - 1 + tm, :]
    y = y + cw[2:3] * u
    mix = gate_b * y
    mo = _mem_attn(qm, mk_ref[0], mv_ref[0])
    out = _dot(mix.astype(BF16), wout_ref[:c, :]) + _dot(mo.astype(BF16), wout_ref[c:, :])
    h_ref[...] = x + out

    @pl.when(i == n_t - 1)
    def _():
        st_ref[0] = ubuf[tm + SUBLANES - 2:tm + SUBLANES, :]


def _mem_spec(n_mem):
    return pl.BlockSpec((1, MEM_W, n_mem), lambda bi, i: (bi, 0, 0))


def _mixa(x2d, b, t, tm, g, win_bf, conv_w, wout_bf, mem_k, mem_v, prev):
    d = x2d.shape[1]
    c = d - MEM_W
    n_t = t // tm
    n_mem = mem_k.shape[2]
    kern = functools.partial(_mixa_kernel, tm=tm, c=c, n_t=n_t)
    return pl.pallas_call(
        kern,
        grid=(b, n_t),
        in_specs=[
            pl.BlockSpec((tm, d), lambda bi, i: (bi * n_t + i, 0)),
            _resident((1, d)),
            _resident(win_bf.shape),
            _resident(conv_w.shape),
            _resident(wout_bf.shape),
            _mem_spec(n_mem),
            _mem_spec(n_mem),
            pl.BlockSpec((1, CONV_WIDTH - 1, c), lambda bi, i: (bi, 0, 0)),
        ],
        out_specs=[
            pl.BlockSpec((tm, d), lambda bi, i: (bi * n_t + i, 0)),
            pl.BlockSpec((1, CONV_WIDTH - 1, c), lambda bi, i: (bi, 0, 0)),
        ],
        out_shape=[
            jax.ShapeDtypeStruct(x2d.shape, F32),
            jax.ShapeDtypeStruct((b, CONV_WIDTH - 1, c), F32),
        ],
        scratch_shapes=[pltpu.VMEM((tm + SUBLANES, c), F32)],
        compiler_params=_cparams("arbitrary", "arbitrary"),
        name="mixa",
    )(x2d, g.reshape(1, d), win_bf, conv_w, wout_bf, mem_k, mem_v, prev)


def _mixa_short_kernel(x_ref, g_ref, win_ref, cw_ref, wout_ref, mk_ref, mv_ref, prev_ref, h_ref, st_ref,
                       *, c, n_seq, t):
    x = x_ref[...]
    hn = _rms(x, g_ref[...]).astype(BF16)
    proj = _dot(hn, win_ref[...])
    cw = cw_ref[...]
    row = lax.broadcasted_iota(jnp.int32, (t, c), 0)
    mixed = []
    for s in range(n_seq):
        p = proj[s * t:(s + 1) * t]
        u = p[:, c:2 * c] * p[:, 2 * c:3 * c]
        prev = prev_ref[s]
        back1 = jnp.where(row == 0, prev[1:2], pltpu.roll(u, 1, 0))
        back2 = jnp.where(row == 0, prev[0:1], jnp.where(row == 1, prev[1:2], pltpu.roll(u, 2, 0)))
        y = cw[0:1] * back2 + cw[1:2] * back1 + cw[2:3] * u
        mo = _mem_attn(p[:, 3 * c:], mk_ref[s], mv_ref[s])
        mixed.append(jnp.concatenate([p[:, :c] * y, mo], axis=1))
        st_ref[s] = u[t - (CONV_WIDTH - 1):]
    h_ref[...] = x + _dot(jnp.concatenate(mixed, axis=0).astype(BF16), wout_ref[...])


def _mixa_short(x2d, b, t, n_seq, g, win_bf, conv_w, wout_bf, mem_k, mem_v, prev):
    d = x2d.shape[1]
    c = d - MEM_W
    n_mem = mem_k.shape[2]
    kern = functools.partial(_mixa_short_kernel, c=c, n_seq=n_seq, t=t)
    seqs = lambda i: (i, 0, 0)
    return pl.pallas_call(
        kern,
        grid=(b // n_seq,),
        in_specs=[
            pl.BlockSpec((n_seq * t, d), lambda i: (i, 0)),
            _resident((1, d)),
            _resident(win_bf.shape),
            _resident(conv_w.shape),
            _resident(wout_bf.shape),
            pl.BlockSpec((n_seq, MEM_W, n_mem), seqs),
            pl.BlockSpec((n_seq, MEM_W, n_mem), seqs),
            pl.BlockSpec((n_seq, CONV_WIDTH - 1, c), seqs),
        ],
        out_specs=[pl.BlockSpec((n_seq * t, d), lambda i: (i, 0)), pl.BlockSpec((n_seq, CONV_WIDTH - 1, c), seqs)],
        out_shape=[jax.ShapeDtypeStruct(x2d.shape, F32), jax.ShapeDtypeStruct((b, CONV_WIDTH - 1, c), F32)],
        compiler_params=_cparams("arbitrary"),
        name="mixa_short",
    )(x2d, g.reshape(1, d), win_bf, conv_w, wout_bf, mem_k, mem_v, prev)


def _silu(g):
    return g / (1.0 + jnp.exp(-g))


def _ffn_kernel(h_ref, g_ref, wg_ref, wu_ref, wd_ref, o_ref):
    h = h_ref[...]
    hn = _rms(h, g_ref[...]).astype(BF16)
    a = _silu(_dot(hn, wg_ref[...])) * _dot(hn, wu_ref[...])
    o_ref[...] = h + _dot(a.astype(BF16), wd_ref[...])


def _ffn(h2d, tm, g, wg_bf, wu_bf, wd_bf):
    m, d = h2d.shape
    return pl.pallas_call(
        _ffn_kernel,
        grid=(m // tm,),
        in_specs=[
            pl.BlockSpec((tm, d), lambda i: (i, 0)),
            _resident((1, d)),
            _resident(wg_bf.shape),
            _resident(wu_bf.shape),
            _resident(wd_bf.shape),
        ],
        out_specs=pl.BlockSpec((tm, d), lambda i: (i, 0)),
        out_shape=jax.ShapeDtypeStruct(h2d.shape, F32),
        compiler_params=_cparams("arbitrary"),
        name="ffn",
    )(h2d, g.reshape(1, d), wg_bf, wu_bf, wd_bf)


def _kvq_kernel(h_ref, gkv_ref, gq_ref, wkv_ref, wq_ref, cos_ref, sin_ref,
                k_ref, v_ref, kb_ref, vb_ref, q_ref, qm_ref, *, n_heads):
    x = h_ref[...]
    xn = x * lax.rsqrt(jnp.mean(x * x, axis=-1, keepdims=True) + EPS)
    cos = cos_ref[...]
    sin = sin_ref[...]
    kv = _dot((xn * gkv_ref[...]).astype(BF16), wkv_ref[...])
    for h in range(n_heads):
        r = _rope_slab(kv[:, h * HEAD_W:(h + 1) * HEAD_W], cos, sin)
        k_ref[0, h] = r
        kb_ref[0, h] = r.astype(BF16)
        v = kv[:, (n_heads + h) * HEAD_W:(n_heads + h + 1) * HEAD_W]
        v_ref[0, h] = v
        vb_ref[0, h] = v.astype(BF16)
    p = _dot((xn * gq_ref[...]).astype(BF16), wq_ref[...])
    for h in range(n_heads):
        q_ref[0, h] = (_rope_slab(p[:, h * HEAD_W:(h + 1) * HEAD_W], cos, sin) * QK_SCALE).astype(BF16)
    qm_ref[...] = p[:, n_heads * HEAD_W:]


def _rope_tables(pos):
    inv = ROPE_THETA ** (-jnp.arange(ROPE_HALF, dtype=F32) / ROPE_HALF)
    ang = pos.astype(F32)[:, None] * inv[None, :]
    cos = jnp.tile(jnp.cos(ang), (1, HEAD_W // ROPE_HALF))
    sin = jnp.sin(ang)
    sin = jnp.tile(jnp.concatenate([-sin, sin], axis=-1), (1, HEAD_W // HALF_W))
    return cos, sin


def _kvqproj(h2d, b, t, tm, g_kv, wkv_bf, g_q, wq_bf, cos, sin):
    m, d = h2d.shape
    n_heads = wkv_bf.shape[1] // (2 * HEAD_W)
    n_t = t // tm
    kern = functools.partial(_kvq_kernel, n_heads=n_heads)
    head_major = pl.BlockSpec((1, n_heads, tm, HEAD_W), lambda bi, i: (bi, 0, i, 0))
    rope = pl.BlockSpec((tm, HEAD_W), lambda bi, i: (i, 0))
    rows = lambda bi, i: (bi * n_t + i, 0)
    hm_f32 = jax.ShapeDtypeStruct((b, n_heads, t, HEAD_W), F32)
    hm_bf16 = jax.ShapeDtypeStruct((b, n_heads, t, HEAD_W), BF16)
    return pl.pallas_call(
        kern,
        grid=(b, n_t),
        in_specs=[pl.BlockSpec((tm, d), rows), _resident((1, d)), _resident((1, d)), _resident(wkv_bf.shape),
                  _resident(wq_bf.shape), rope, rope],
        out_specs=[head_major] * 5 + [pl.BlockSpec((tm, MEM_W), rows)],
        out_shape=[hm_f32, hm_f32, hm_bf16, hm_bf16, hm_bf16, jax.ShapeDtypeStruct((m, MEM_W), F32)],
        compiler_params=_cparams("arbitrary", "arbitrary"),
        name="kvqproj",
    )(h2d, g_kv.reshape(1, d), g_q.reshape(1, d), wkv_bf, wq_bf, cos, sin)


def _stack_components(q):
    lane = lax.broadcasted_iota(jnp.int32, q.shape, 1)
    zero = jnp.zeros_like(q)
    return jnp.concatenate([jnp.where(lane < HALF_W, q, zero), jnp.where(lane >= HALF_W, q, zero)], axis=0)


def _online_update(s, pv, m_ref, l_ref, acc_ref):
    cols = [s[:, c * LANES:(c + 1) * LANES] for c in range(s.shape[1] // LANES)]
    m_prev = m_ref[...]
    m_new = jnp.maximum(m_prev, jnp.max(functools.reduce(jnp.maximum, cols), axis=-1, keepdims=True))
    alpha = jnp.exp2(m_prev - m_new)
    ps = [jnp.exp2(c - m_new) for c in cols]
    l_ref[...] = alpha * l_ref[...] + functools.reduce(jnp.add, ps)
    acc_ref[...] = alpha * acc_ref[...] + pv(jnp.concatenate(ps, axis=1).astype(BF16))
    m_ref[...] = m_new


def _init_online(m_ref, l_ref, acc_ref):
    m_ref[...] = jnp.full(m_ref.shape, NEG, F32)
    l_ref[...] = jnp.zeros(l_ref.shape, F32)
    acc_ref[...] = jnp.zeros(acc_ref.shape, F32)


def _diff_combine(l_ref, acc_ref, t, lam):
    o = acc_ref[...] / jnp.sum(l_ref[...], axis=-1, keepdims=True)
    return o[:t] - lam * o[t:]


def _attn_prompt_kernel(q_ref, k_ref, v_ref, lam_ref, sg_ref, o_ref, m_ref, l_ref, acc_ref, *, tq, lam_init):
    qi = pl.program_id(2)
    qs = _stack_components(q_ref[0, 0])
    _init_online(m_ref, l_ref, acc_ref)

    def chunk(j, masked):
        start = pl.multiple_of(j * tq, tq)
        s = _dot_t(qs, k_ref[0, 0, pl.ds(start, tq), :])
        if masked:
            row = lax.broadcasted_iota(jnp.int32, s.shape, 0) % tq
            col = lax.broadcasted_iota(jnp.int32, s.shape, 1)
            s = jnp.where(col <= row, s, NEG)
        v = v_ref[0, 0, pl.ds(start, tq), :]
        _online_update(s, lambda p: _dot(p, v), m_ref, l_ref, acc_ref)

    def body(j, carry):
        chunk(2 * j, False)
        chunk(2 * j + 1, False)
        return carry

    lax.fori_loop(0, qi // 2, body, 0)

    @pl.when(qi % 2 == 1)
    def _():
        chunk(qi - 1, False)

    chunk(qi, True)
    o = _diff_combine(l_ref, acc_ref, tq, _lam(lam_ref, lam_init))
    o_ref[...] = _subln(o, sg_ref[...], lam_init).astype(BF16)


def _attn_prompt(q_bf, k_bf, v_bf, tq, lam_p, subln_g, lam_init):
    b, n_heads, t, _ = q_bf.shape
    nq = t // tq
    kern = functools.partial(_attn_prompt_kernel, tq=tq, lam_init=lam_init)
    seq = pl.BlockSpec((1, 1, t, HEAD_W), lambda bi, h, qi: (bi, h, 0, 0))
    return pl.pallas_call(
        kern,
        grid=(b, n_heads, nq),
        in_specs=[
            pl.BlockSpec((1, 1, tq, HEAD_W), lambda bi, h, qi: (bi, h, qi, 0)),
            seq,
            seq,
            pl.BlockSpec(lam_p.shape, lambda bi, h, qi: (0, 0)),
            pl.BlockSpec((1, HEAD_W), lambda bi, h, qi: (0, 0)),
        ],
        out_specs=pl.BlockSpec((tq, HEAD_W), lambda bi, h, qi: (bi * nq + qi, h)),
        out_shape=jax.ShapeDtypeStruct((b * t, n_heads * HEAD_W), BF16),
        scratch_shapes=[pltpu.VMEM((2 * tq, LANES), F32)] * 3,
        compiler_params=_cparams("arbitrary", "arbitrary", "arbitrary"),
        name="attn_prompt",
    )(q_bf, k_bf, v_bf, lam_p, subln_g.reshape(1, HEAD_W))


def _attn_sample_kernel(pt_ref, q_ref, kn_ref, vn_ref, *rest, ts, n_heads, pages, n_groups, lam_init):
    k_refs = rest[:pages]
    v_refs = rest[pages:2 * pages]
    lam_ref, sg_ref, o_ref, qs_ref, m_ref, l_ref, acc_ref = rest[2 * pages:]
    g = pl.program_id(1)

    r = 2 * ts

    @pl.when(g == 0)
    def _():
        for h in range(n_heads):
            qs_ref[h] = _stack_components(q_ref[0, h])
        _init_online(m_ref, l_ref, acc_ref)

    def all_heads(keys, values):
        s = jnp.concatenate([_dot_t(qs_ref[h], keys(h)) for h in range(n_heads)], axis=0)

        def pv(p):
            return jnp.concatenate([_dot(p[h * r:(h + 1) * r], values(h)) for h in range(n_heads)], axis=0)

        return s, pv

    def past(refs):
        return lambda h: jnp.concatenate([x[0, h].astype(BF16) for x in refs], axis=0)

    s, pv = all_heads(past(k_refs), past(v_refs))
    _online_update(s, pv, m_ref, l_ref, acc_ref)

    @pl.when(g == n_groups - 1)
    def _():
        pad = jnp.zeros((LANES - ts, HEAD_W), BF16)

        def new(ref):
            return lambda h: jnp.concatenate([ref[0, h].astype(BF16), pad], axis=0)

        s, pv = all_heads(new(kn_ref), new(vn_ref))
        row = lax.broadcasted_iota(jnp.int32, s.shape, 0) % ts
        col = lax.broadcasted_iota(jnp.int32, s.shape, 1)
        _online_update(jnp.where(col <= row, s, NEG), pv, m_ref, l_ref, acc_ref)
        lam = _lam(lam_ref, lam_init)
        for h in range(n_heads):
            rows = pl.ds(h * r, r)
            o = _diff_combine(l_ref.at[rows], acc_ref.at[rows], ts, lam)
            o_ref[:, h * HEAD_W:(h + 1) * HEAD_W] = _subln(o, sg_ref[...], lam_init).astype(BF16)


def _attn_sample(q_bf, k_new, v_new, cache_k, cache_v, page_table, pages, lam_p, subln_g, lam_init):
    bs, n_heads, ts, _ = q_bf.shape
    ps = cache_k.shape[2]
    n_groups = page_table.shape[1] // pages
    kern = functools.partial(_attn_sample_kernel, ts=ts, n_heads=n_heads, pages=pages, n_groups=n_groups,
                             lam_init=lam_init)

    def page_spec(i):
        return pl.BlockSpec((1, n_heads, ps, HEAD_W), lambda bi, g, pt: (pt[bi, g * pages + i], 0, 0, 0))

    new_rows = pl.BlockSpec((1, n_heads, ts, HEAD_W), lambda bi, g, pt: (bi, 0, 0, 0))
    grid_spec = pltpu.PrefetchScalarGridSpec(
        num_scalar_prefetch=1,
        grid=(bs, n_groups),
        in_specs=[new_rows] * 3
        + [page_spec(i) for i in range(pages)] * 2
        + [pl.BlockSpec(lam_p.shape, lambda bi, g, pt: (0, 0)), pl.BlockSpec((1, HEAD_W), lambda bi, g, pt: (0, 0))],
        out_specs=pl.BlockSpec((ts, n_heads * HEAD_W), lambda bi, g, pt: (bi, 0)),
        scratch_shapes=[pltpu.VMEM((n_heads, 2 * ts, HEAD_W), BF16)]
        + [pltpu.VMEM((n_heads * 2 * ts, LANES), F32)] * 3,
    )
    return pl.pallas_call(
        kern,
        grid_spec=grid_spec,
        out_shape=jax.ShapeDtypeStruct((bs * ts, n_heads * HEAD_W), BF16),
        compiler_params=_cparams("arbitrary", "arbitrary"),
        name="attn_sample",
    )(page_table, q_bf, k_new, v_new, *([cache_k] * pages), *([cache_v] * pages), lam_p, subln_g.reshape(1, HEAD_W))


def _route(h, g, r_hi, r_lo, base, n_experts):
    hn = _rms(h, g)
    hi = hn.astype(BF16)
    lo = (hn - hi.astype(F32)).astype(BF16)
    logits = _dot(hi, r_hi) + (_dot(hi, r_lo) + _dot(lo, r_hi))
    lane = lax.broadcasted_iota(jnp.int32, logits.shape, 1)
    logits = jnp.where(lane < n_experts, logits, -jnp.inf)
    v1 = jnp.max(logits, axis=-1, keepdims=True)
    e1 = jnp.min(jnp.where(logits == v1, lane, LANES), axis=-1, keepdims=True)
    rest = jnp.where(lane == e1, -jnp.inf, logits)
    v2 = jnp.max(rest, axis=-1, keepdims=True)
    e2 = jnp.min(jnp.where(rest == v2, lane, LANES), axis=-1, keepdims=True)
    ex = jnp.exp(v2 - v1)
    g1 = 1.0 / (1.0 + ex)
    g2 = ex / (1.0 + ex)
    sel = jnp.where((lane == e1) | (lane == e2), 1.0, 0.0)
    t = h.shape[0]
    earlier = lax.broadcasted_iota(jnp.int32, (t, t), 0) > lax.broadcasted_iota(jnp.int32, (t, t), 1)
    cum = _dot(jnp.where(earlier, 1.0, 0.0).astype(BF16), sel.astype(BF16)) + base
    r1 = jnp.sum(jnp.where(lane == e1, cum, 0.0), axis=-1, keepdims=True)
    r2 = jnp.sum(jnp.where(lane == e2, cum, 0.0), axis=-1, keepdims=True)
    out = jnp.where(lane == 0, e1.astype(F32), 0.0)
    for i, val in enumerate((e2.astype(F32), g1, g2, r1, r2), start=1):
        out = jnp.where(lane == i, val, out)
    return out, sel


def _mixb_kernel(mix_ref, qm_ref, x_ref, wout_ref, mk_ref, mv_ref, gf_ref, rhi_ref, rlo_ref,
                 h_ref, route_ref, cnt_ref, carry, *, n_experts, n_seq):
    @pl.when((pl.program_id(0) == 0) & (pl.program_id(1) == 0))
    def _():
        carry[...] = jnp.zeros(carry.shape, F32)

    c = mix_ref.shape[1]
    qm = qm_ref[...]
    t = qm.shape[0] // n_seq
    mo = jnp.concatenate([_mem_attn(qm[s * t:(s + 1) * t], mk_ref[s], mv_ref[s]) for s in range(n_seq)], axis=0)
    h = x_ref[...] + (_dot(mix_ref[...], wout_ref[:c, :]) + _dot(mo.astype(BF16), wout_ref[c:, :]))
    h_ref[...] = h
    route, sel = _route(h, gf_ref[...], rhi_ref[...], rlo_ref[...], carry[...], n_experts)
    route_ref[...] = route
    carry[...] += jnp.sum(sel, axis=0, keepdims=True)
    cnt_ref[...] = carry[...]


def _mixb(mix_bf, qm, x2d, b, t, tm, wout_bf, mem_k, mem_v, g_ffn, router, n_seq=1):
    m, d = x2d.shape
    c = mix_bf.shape[1]
    assert n_seq == 1 or tm == t
    n_t = t // tm
    tm = tm * n_seq
    n_mem = mem_k.shape[2]
    n_experts = router.shape[1]
    r_pad = jnp.zeros((d, LANES), F32).at[:, :n_experts].set(router)
    r_hi = r_pad.astype(BF16)
    r_lo = (r_pad - r_hi.astype(F32)).astype(BF16)
    row = lambda bi, i: (bi * n_t + i, 0)
    mem = pl.BlockSpec((n_seq, MEM_W, n_mem), lambda bi, i: (bi, 0, 0))
    kern = functools.partial(_mixb_kernel, n_experts=n_experts, n_seq=n_seq)
    return pl.pallas_call(
        kern,
        grid=(b // n_seq, n_t),
        in_specs=[
            pl.BlockSpec((tm, c), row),
            pl.BlockSpec((tm, MEM_W), row),
            pl.BlockSpec((tm, d), row),
            _resident(wout_bf.shape),
            mem,
            mem,
            _resident((1, d)),
            _resident((d, LANES)),
            _resident((d, LANES)),
        ],
        out_specs=[
            pl.BlockSpec((tm, d), row),
            pl.BlockSpec((tm, LANES), row),
            pl.BlockSpec((1, LANES), lambda bi, i: (0, 0)),
        ],
        out_shape=[
            jax.ShapeDtypeStruct((m, d), F32),
            jax.ShapeDtypeStruct((m, LANES), F32),
            jax.ShapeDtypeStruct((1, LANES), F32),
        ],
        scratch_shapes=[pltpu.VMEM((1, LANES), F32)],
        compiler_params=_cparams("arbitrary", "arbitrary"),
        name="mixb",
    )(mix_bf, qm, x2d, wout_bf, mem_k, mem_v, g_ffn.reshape(1, d), r_hi, r_lo)


GATHER_UNROLL = 8
SEQS_PER_STEP = 8


def _row_copy(src_hbm, row, dst, slot, sem):
    return pltpu.make_async_copy(src_hbm.at[pl.ds(row, 1)], dst.at[pl.ds(slot, 1)], sem)


def _gather_start(src_hbm, idx_ref, base, dst, sem, n):
    def issue(j, carry):
        for u in range(GATHER_UNROLL):
            r = j * GATHER_UNROLL + u
            _row_copy(src_hbm, idx_ref[base + r], dst, r, sem).start(priority=u % 2)
        return carry

    lax.fori_loop(0, n // GATHER_UNROLL, issue, 0)


def _gather_start_inline(src_hbm, idx_ref, base, dst, sem, n):
    for r in range(n):
        _row_copy(src_hbm, idx_ref[base + r], dst, r, sem).start(priority=r % 2)


def _gather_wait(src_hbm, dst, sem, n):
    def wait(r, carry):
        _row_copy(src_hbm, 0, dst, r, sem).wait()
        return carry

    lax.fori_loop(0, n, wait, 0, unroll=GATHER_UNROLL)


def _expert_kernel(blk_e_ref, tok_ref, nused_ref, h_hbm, g_ref, wg_ref, wu_ref, wd_ref, o_ref, xbuf, sem, *, blk):
    i = pl.program_id(0)
    n_used = nused_ref[0]
    slot = i % 2

    @pl.when(i == 0)
    def _():
        _gather_start(h_hbm, tok_ref, 0, xbuf.at[0], sem.at[0], blk)

    @pl.when(i + 1 < n_used)
    def _():
        _gather_start(h_hbm, tok_ref, (i + 1) * blk, xbuf.at[1 - slot], sem.at[1 - slot], blk)

    @pl.when(i < n_used)
    def _():
        _gather_wait(h_hbm, xbuf.at[slot], sem.at[slot], blk)
        hn = _rms(xbuf[slot], g_ref[...]).astype(BF16)
        a = _silu(_dot(hn, wg_ref[0])) * _dot(hn, wu_ref[0])
        o_ref[...] = _dot(a.astype(BF16), wd_ref[0])

    @pl.when(i >= n_used)
    def _():
        o_ref[...] = jnp.zeros(o_ref.shape, F32)


def _experts(h2d, g, wg_bf, wu_bf, wd_bf, blk_e, buf_tok, n_used, blk):
    n_rows = buf_tok.shape[0]
    n_blk = n_rows // blk
    _, d, ff = wg_bf.shape
    kern = functools.partial(_expert_kernel, blk=blk)
    grid_spec = pltpu.PrefetchScalarGridSpec(
        num_scalar_prefetch=3,
        grid=(n_blk,),
        in_specs=[
            pl.BlockSpec(memory_space=pl.ANY),
            pl.BlockSpec((1, d), lambda i, be, tk, nu: (0, 0)),
            pl.BlockSpec((1, d, ff), lambda i, be, tk, nu: (be[i], 0, 0)),
            pl.BlockSpec((1, d, ff), lambda i, be, tk, nu: (be[i], 0, 0)),
            pl.BlockSpec((1, ff, d), lambda i, be, tk, nu: (be[i], 0, 0)),
        ],
        out_specs=pl.BlockSpec((blk, d), lambda i, be, tk, nu: (i, 0)),
        scratch_shapes=[pltpu.VMEM((2, blk, d), F32), pltpu.SemaphoreType.DMA((2,))],
    )
    return pl.pallas_call(
        kern,
        grid_spec=grid_spec,
        out_shape=jax.ShapeDtypeStruct((n_rows, d), F32),
        compiler_params=_cparams("arbitrary"),
        name="experts",
    )(blk_e, buf_tok, n_used, h2d, g.reshape(1, d), wg_bf, wu_bf, wd_bf)


def _combine_kernel(pos_ref, h_ref, route_ref, g_ref, yb_hbm, o_ref, ybuf, sem, *, tc, n_steps):
    i = pl.program_id(0)
    slot = i % 2

    @pl.when(i == 0)
    def _():
        for k in range(TOP_K):
            _gather_start(yb_hbm, pos_ref, k * n_steps * tc, ybuf.at[0, k], sem.at[0], tc)

    for k in range(TOP_K):
        _gather_wait(yb_hbm, ybuf.at[slot, k], sem.at[slot], tc)
    nxt = jnp.minimum(i + 1, n_steps - 1)
    for k in range(TOP_K):
        _gather_start_inline(yb_hbm, pos_ref, (k * n_steps + nxt) * tc, ybuf.at[1 - slot, k], sem.at[1 - slot], tc)
    route = route_ref[...]
    y = ybuf[slot, 0] * route[:, 2:3] + ybuf[slot, 1] * route[:, 3:4]
    o_ref[...] = _rms(h_ref[...] + y, g_ref[...])

    @pl.when(i == n_steps - 1)
    def _():
        for k in range(TOP_K):
            _gather_wait(yb_hbm, ybuf.at[1 - slot, k], sem.at[1 - slot], tc)


def _combine(h2d, route, yb, pos_k_major, g, tc):
    m, d = h2d.shape
    n_steps = m // tc
    kern = functools.partial(_combine_kernel, tc=tc, n_steps=n_steps)
    grid_spec = pltpu.PrefetchScalarGridSpec(
        num_scalar_prefetch=1,
        grid=(n_steps,),
        in_specs=[
            pl.BlockSpec((tc, d), lambda i, p: (i, 0)),
            pl.BlockSpec((tc, LANES), lambda i, p: (i, 0)),
            pl.BlockSpec((1, d), lambda i, p: (0, 0)),
            pl.BlockSpec(memory_space=pl.ANY),
        ],
        out_specs=pl.BlockSpec((tc, d), lambda i, p: (i, 0)),
        scratch_shapes=[pltpu.VMEM((2, TOP_K, tc, d), F32), pltpu.SemaphoreType.DMA((2,))],
    )
    return pl.pallas_call(
        kern,
        grid_spec=grid_spec,
        out_shape=jax.ShapeDtypeStruct((m, d), F32),
        compiler_params=_cparams("arbitrary"),
        name="combine",
    )(pos_k_major, h2d, route, g.reshape(1, d), yb)


def _slot_tokens_kernel(dest_ref, pad_lo_ref, pad_hi_ref, o_ref, *, n_assign, n_pads):
    def zero(p, carry):
        o_ref[p] = 0
        return carry

    for e in range(n_pads):
        lax.fori_loop(pad_lo_ref[e], pad_hi_ref[e], zero, 0)

    def put(j, carry):
        for u in range(GATHER_UNROLL):
            tok = j * GATHER_UNROLL + u
            for k in range(TOP_K):
                o_ref[dest_ref[tok * TOP_K + k]] = tok
        return carry

    lax.fori_loop(0, n_assign // (TOP_K * GATHER_UNROLL), put, 0)


def _slot_tokens(dest_flat, pad_lo, pad_hi, n_rows):
    n_assign = dest_flat.shape[0]
    kern = functools.partial(_slot_tokens_kernel, n_assign=n_assign, n_pads=pad_lo.shape[0])
    grid_spec = pltpu.PrefetchScalarGridSpec(
        num_scalar_prefetch=3, grid=(1,), in_specs=[], out_specs=pl.BlockSpec(memory_space=pltpu.SMEM))
    return pl.pallas_call(
        kern,
        grid_spec=grid_spec,
        out_shape=jax.ShapeDtypeStruct((n_rows,), jnp.int32),
        compiler_params=_cparams("arbitrary"),
        name="slot_tokens",
    )(dest_flat, pad_lo, pad_hi)


def _moe(h2d, route, counts, tc, g_ffn, wg_bf, wu_bf, wd_bf, g_final, blk):
    m, d = h2d.shape
    n_experts = wg_bf.shape[0]
    a = m * TOP_K
    counts = counts[0, :n_experts].astype(jnp.int32)
    padded = (counts + blk - 1) // blk * blk
    cum_pad = jnp.cumsum(padded)
    top_e = route[:, :TOP_K].astype(jnp.int32)
    dest = (cum_pad - padded)[top_e] + route[:, 2 * TOP_K:3 * TOP_K].astype(jnp.int32)
    n_blk = -(-(a + n_experts * (blk - 1)) // blk)
    n_rows = n_blk * blk
    pad_lo = jnp.concatenate([cum_pad - padded + counts, cum_pad[-1:]]).astype(jnp.int32)
    pad_hi = jnp.concatenate([cum_pad, jnp.full((1,), n_rows, cum_pad.dtype)]).astype(jnp.int32)
    buf_tok = _slot_tokens(dest.reshape(-1), pad_lo, pad_hi, n_rows)
    n_used = (cum_pad[-1] // blk).astype(jnp.int32)
    blk_i = jnp.minimum(jnp.arange(n_blk, dtype=jnp.int32), n_used - 1)
    blk_e = jnp.sum((blk_i[:, None] * blk >= cum_pad[None, :]).astype(jnp.int32), axis=1)
    blk_e = jnp.minimum(blk_e, n_experts - 1)
    yb = _experts(h2d, g_ffn, wg_bf, wu_bf, wd_bf, blk_e, buf_tok, n_used.reshape(1), blk)
    return _combine(h2d, route, yb, dest.T.reshape(-1), g_final, tc)


def _head_major(x, b, t):
    n_heads = x.shape[1]
    return jnp.transpose(x.reshape(n_heads, b, t, HEAD_W), (1, 0, 2, 3))


def _trunk(x, pos, conv_prev, mem_k, mem_v, attend, w, tm_a, tm, tok_tm, moe_blk):
    b, t, d = x.shape
    x2d = x.reshape(b * t, d)
    short = t == SUBLANES
    n_seq = math.gcd(b, SEQS_PER_STEP) if short else 1
    mix_a = (w['g_mix'][0], w['a_w_in'], w['a_conv'], w['a_w_out'], mem_k[0], mem_v[0], conv_prev)
    h, conv_st = _mixa_short(x2d, b, t, n_seq, *mix_a) if short else _mixa(x2d, b, t, tm_a, *mix_a)
    h = _ffn(h, tok_tm, w['g_ffn'][0], w['f_w_gate'], w['f_w_up'], w['f_w_down'])
    cos, sin = _rope_tables(pos)
    proj = (w['g_kv'], w['w_kv'], w['g_mix'][1], w['b_w_in'])
    if tok_tm > t:
        cos, sin = jnp.tile(cos, (b, 1)), jnp.tile(sin, (b, 1))
        *heads, qm = _kvqproj(h, 1, b * t, tok_tm, *proj, cos, sin)
        k, v, k_bf, v_bf, q_bf = (_head_major(o, b, t) for o in heads)
    else:
        k, v, k_bf, v_bf, q_bf, qm = _kvqproj(h, b, t, tok_tm, *proj, cos, sin)
    mix = attend(q_bf, k, v, k_bf, v_bf)
    h, route, counts = _mixb(mix, qm, h, b, t, tm, w['b_w_out'], mem_k[1], mem_v[1], w['g_ffn'][1], w['m_router'],
                             n_seq)
    y = _moe(h, route, counts, tok_tm, w['g_ffn'][1], w['m_w_gate'], w['m_w_up'], w['m_w_down'], w['g_final'],
             moe_blk)
    return y.reshape(b, t, d), conv_st, k, v


def _token_major(x):
    return jnp.transpose(x, (0, 2, 1, 3))


def kernel(x_prompt, x_sample, state_conv, cache_k, cache_v, cache_mem_k, cache_mem_v, page_table, mem_prompt, g_mix, g_ffn, g_mem, w_mem_kv, a_w_in, a_conv, a_w_out, g_kv, w_kv, b_w_in, b_lambda, b_subln, b_w_out, f_w_gate, f_w_up, f_w_down, m_router, m_w_gate, m_w_up, m_w_down, g_final):
    bp, tp, d = x_prompt.shape
    bs, ts, _ = x_sample.shape
    depth = g_mix.shape[0]
    n_a = a_w_in.shape[0]
    assert depth == 2 and n_a == 1 and b_w_in.shape[0] == 1, "one conv layer followed by one attention layer"
    assert ts == SUBLANES and tp % 512 == 0
    c = d - MEM_W
    lam_init = 0.8 - 0.6 * math.exp(-0.3 * n_a)
    w = {
        'g_mix': g_mix, 'g_ffn': g_ffn, 'g_kv': g_kv, 'g_final': g_final,
        'a_w_in': a_w_in[0].astype(BF16), 'a_conv': a_conv[0], 'a_w_out': a_w_out[0].astype(BF16),
        'w_kv': w_kv.astype(BF16), 'b_w_in': b_w_in[0].astype(BF16), 'b_w_out': b_w_out[0].astype(BF16),
        'f_w_gate': f_w_gate[0].astype(BF16), 'f_w_up': f_w_up[0].astype(BF16), 'f_w_down': f_w_down[0].astype(BF16),
        'm_router': m_router[0], 'm_w_gate': m_w_gate[0].astype(BF16), 'm_w_up': m_w_up[0].astype(BF16),
        'm_w_down': m_w_down[0].astype(BF16),
    }
    lam_p = b_lambda[0]
    subln_g = b_subln[0]

    n_mem = mem_prompt.shape[1]
    mem_kt, mem_vt = _memkv(mem_prompt, g_mem, w_mem_kv.astype(BF16))

    def mem_out(x):
        return jnp.transpose(x.reshape(depth, bp, MEM_HEADS, MEM_HEAD_DIM, n_mem), (0, 1, 4, 2, 3))

    def attend_prompt(q_bf, k, v, k_bf, v_bf):
        return _attn_prompt(q_bf, k_bf, v_bf, 512, lam_p, subln_g, lam_init)

    y_p, conv_p, k_p, v_p = _trunk(
        x_prompt, jnp.arange(tp), jnp.zeros((bp, CONV_WIDTH - 1, c), F32), mem_kt, mem_vt, attend_prompt, w,
        tm_a=512, tm=256, tok_tm=512, moe_blk=256)

    past = page_table.shape[1] * cache_k.shape[1]
    pages = math.gcd(page_table.shape[1], 16)
    cache_kh = jnp.transpose(cache_k, (0, 2, 1, 3))
    cache_vh = jnp.transpose(cache_v, (0, 2, 1, 3))

    def mem_in(x):
        return jnp.transpose(x, (0, 1, 3, 4, 2)).reshape(depth, bs, MEM_W, x.shape[2])

    def attend_sample(q_bf, k, v, k_bf, v_bf):
        return _attn_sample(q_bf, k, v, cache_kh, cache_vh, page_table, pages, lam_p, subln_g, lam_init)

    y_s, conv_s, k_s, v_s = _trunk(
        x_sample, past + jnp.arange(ts), state_conv[0], mem_in(cache_mem_k), mem_in(cache_mem_v), attend_sample, w,
        tm_a=ts, tm=ts, tok_tm=bs * ts, moe_blk=128)

    return (y_p, y_s, conv_p[None], conv_s[None], _token_major(k_p), _token_major(v_p), _token_major(k_s),
            _token_major(v_s), mem_out(mem_kt), mem_out(mem_vt))
```
